```python
import math
import jax, jax.numpy as jnp
from jax import lax
import numpy as np

D_MODEL = 1024
BATCH = 8
SEQ = 2048
DEPTH = 1

CHUNK = 64
MEM_LEN = 256
Q_BLOCK = 128
EPS = 1e-6
RET_HEADS = 4
RET_DK = D_MODEL // RET_HEADS
RET_DV = 2 * RET_DK
RET_THETA_BASE = 10000.0
DIFF_HEADS = 8
DIFF_DK = D_MODEL // (2 * DIFF_HEADS)
DIFF_DV = 2 * DIFF_DK
ROPE_THETA = 500000.0
ROT_DIM = DIFF_DK // 4
LAMBDA_INIT_SCALE = 0.1
MEM_HEADS = 4
MEM_DH = D_MODEL // MEM_HEADS
D_FF = 2816
N_BRANCHES = 3

RET_QK_W = RET_HEADS * RET_DK
RET_V_W = RET_HEADS * RET_DV
DIFF_QK_W = DIFF_HEADS * 2 * DIFF_DK
DIFF_V_W = DIFF_HEADS * DIFF_DV
MEM_Q_W = MEM_HEADS * MEM_DH
GATE_W = N_BRANCHES * D_MODEL
IN_SPLITS = (RET_QK_W, RET_QK_W, RET_V_W, RET_V_W, DIFF_QK_W, DIFF_QK_W, DIFF_V_W, MEM_Q_W, GATE_W)
D_IN = int(sum(IN_SPLITS))
IN_OFFSETS = tuple(int(o) for o in np.cumsum(IN_SPLITS)[:-1])

kernel_name = "hybrid_retention_diffattn_memory_macaron"


def _rms(x, w=None):
    xf = x.astype(jnp.float32)
    y = xf * lax.rsqrt(jnp.mean(xf * xf, axis=-1, keepdims=True) + EPS)
    if w is not None:
        y = y * w.astype(jnp.float32)
    return y.astype(x.dtype)


def _swiglu(h, wg, wu, wd):
    return (jax.nn.silu(h @ wg) * (h @ wu)) @ wd


def _rope_partial(x, cos, sin):
    x1 = x[..., :ROT_DIM // 2]
    x2 = x[..., ROT_DIM // 2:ROT_DIM]
    return jnp.concatenate([x1 * cos - x2 * sin, x2 * cos + x1 * sin, x[..., ROT_DIM:]], axis=-1)


def _rot_interleaved(x, cos, sin):
    xe = x[..., 0::2]
    xo = x[..., 1::2]
    return jnp.stack([xe * cos - xo * sin, xo * cos + xe * sin], axis=-1).reshape(x.shape)


def _retention(q, k, v):
    B, S = q.shape[0], q.shape[1]
    N = S // CHUNK
    dt = v.dtype
    log_g = jnp.log(1.0 - 2.0 ** (-5.0 - jnp.arange(RET_HEADS, dtype=jnp.float32)))
    idx = jnp.arange(CHUNK, dtype=jnp.float32)
    d_intra = jnp.exp(log_g[:, None, None] * jnp.abs(idx[:, None] - idx[None, :])).astype(dt)
    q_dec = jnp.exp(log_g[:, None] * (idx[None, :] + 1.0)).astype(dt)
    k_dec = jnp.exp(log_g[:, None] * (CHUNK - 1.0 - idx[None, :])).astype(dt)
    c_dec = jnp.exp(log_g * CHUNK).astype(dt)

    def to_chunks(t):
        return t.reshape(B, N, CHUNK, RET_HEADS, t.shape[-1]).transpose(1, 0, 3, 2, 4)

    qc, kc, vc = to_chunks(q), to_chunks(k), to_chunks(v)
    scores = jnp.einsum('nbhcd,nbhed->nbhce', qc, kc) * d_intra
    o_intra = jnp.einsum('nbhce,nbhef->nbhcf', scores, vc)

    def step(state, xs):
        qn, kn, vn = xs
        o = jnp.einsum('bhcd,bhde->bhce', qn * q_dec[None, :, :, None], state)
        state = state * c_dec[None, :, None, None] + jnp.einsum(
            'bhcd,bhce->bhde', kn * k_dec[None, :, :, None], vn)
        return state, o

    s0 = jnp.zeros((B, RET_HEADS, RET_DK, RET_DV), dt)
    _, o_inter = lax.scan(step, s0, (qc, kc, vc))
    o = (o_intra + o_inter).transpose(1, 0, 3, 2, 4)
    return o.reshape(B, S, RET_HEADS, RET_DV)


def _diff_attention(q, k, v, lam):
    S = q.shape[1]
    scale = 1.0 / math.sqrt(DIFF_DK)
    q = q.transpose(0, 2, 3, 1, 4)
    k = k.transpose(0, 2, 3, 1, 4)
    v = v.transpose(0, 2, 1, 3)
    chunk_id = jnp.arange(S) // CHUNK
    outs = []
    for i in range(S // Q_BLOCK):
        q0, q1 = i * Q_BLOCK, (i + 1) * Q_BLOCK
        s = jnp.einsum('bhrqd,bhrkd->bhrqk', q[:, :, :, q0:q1], k[:, :, :, :q1]).astype(jnp.float32) * scale
        mask = chunk_id[q0:q1, None] >= chunk_id[None, :q1]
        p = jax.nn.softmax(jnp.where(mask, s, -jnp.inf), axis=-1)
        a = p[:, :, 0] - lam * p[:, :, 1]
        outs.append(jnp.einsum('bhqk,bhkd->bhqd', a.astype(v.dtype), v[:, :, :q1]))
    return jnp.concatenate(outs, axis=2).transpose(0, 2, 1, 3)


def _mem_attention(q, k, v):
    s = jnp.einsum('bshd,bmhd->bhsm', q, k).astype(jnp.float32) * (1.0 / math.sqrt(MEM_DH))
    p = jax.nn.softmax(s, axis=-1)
    return jnp.einsum('bhsm,bmhd->bshd', p.astype(v.dtype), v)


def setup_inputs(seed: int = 0) -> dict:
    key = jax.random.key(seed)
    ks = jax.random.split(key, 32)
    f32 = jnp.float32
    L = DEPTH

    def w(k, shape, fan_in):
        return jax.random.normal(k, shape, f32) * (fan_in ** -0.5)

    def gain(k, shape):
        return 1.0 + 0.05 * jax.random.normal(k, shape, f32)

    offset = jax.random.randint(ks[2], (BATCH, 1), 0, 64) * CHUNK
    positions = (offset + jnp.arange(SEQ)[None, :]).astype(jnp.int32)
    return {
        "x": jax.random.normal(ks[0], (BATCH, SEQ, D_MODEL), f32),
        "mem": jax.random.normal(ks[1], (BATCH, MEM_LEN, D_MODEL), f32),
        "positions": positions,
        "ffn1_norm": gain(ks[3], (L, D_MODEL)),
        "ffn1_w_gate": w(ks[4], (L, D_MODEL, D_FF), D_MODEL),
        "ffn1_w_up": w(ks[5], (L, D_MODEL, D_FF), D_MODEL),
        "ffn1_w_down": w(ks[6], (L, D_FF, D_MODEL), D_FF),
        "mix_norm": gain(ks[7], (L, D_MODEL)),
        "w_in": w(ks[8], (L, D_MODEL, D_IN), D_MODEL),
        "b_gate": 0.01 * jax.random.normal(ks[9], (L, GATE_W), f32),
        "ret_w_o": w(ks[10], (L, RET_V_W, D_MODEL), RET_V_W),
        "diff_q_norm": gain(ks[11], (L, DIFF_DK)),
        "diff_k_norm": gain(ks[12], (L, DIFF_DK)),
        "diff_lambda_q1": LAMBDA_INIT_SCALE * jax.random.normal(ks[13], (L, DIFF_DK), f32),
        "diff_lambda_k1": LAMBDA_INIT_SCALE * jax.random.normal(ks[14], (L, DIFF_DK), f32),
        "diff_lambda_q2": LAMBDA_INIT_SCALE * jax.random.normal(ks[15], (L, DIFF_DK), f32),
        "diff_lambda_k2": LAMBDA_INIT_SCALE * jax.random.normal(ks[16], (L, DIFF_DK), f32),
        "diff_subln": gain(ks[17], (L, DIFF_DV)),
        "diff_w_o": w(ks[18], (L, DIFF_V_W, D_MODEL), DIFF_V_W),
        "mem_norm": gain(ks[19], (L, D_MODEL)),
        "mem_w_kv": w(ks[20], (L, D_MODEL, 2 * MEM_Q_W), D_MODEL),
        "mem_q_norm": gain(ks[21], (L, MEM_DH)),
        "mem_k_norm": gain(ks[22], (L, MEM_DH)),
        "mem_w_o": w(ks[23], (L, MEM_Q_W, D_MODEL), MEM_Q_W),
        "w_out": w(ks[24], (L, D_MODEL, D_MODEL), D_MODEL),
        "ffn2_norm": gain(ks[25], (L, D_MODEL)),
        "ffn2_w_gate": w(ks[26], (L, D_MODEL, D_FF), D_MODEL),
        "ffn2_w_up": w(ks[27], (L, D_MODEL, D_FF), D_MODEL),
        "ffn2_w_down": w(ks[28], (L, D_FF, D_MODEL), D_FF),
        "final_norm": gain(ks[29], (L, D_MODEL)),
    }


def reference(x, mem, positions, ffn1_norm, ffn1_w_gate, ffn1_w_up, ffn1_w_down, mix_norm, w_in, b_gate,
              ret_w_o, diff_q_norm, diff_k_norm, diff_lambda_q1, diff_lambda_k1, diff_lambda_q2,
              diff_lambda_k2, diff_subln, diff_w_o, mem_norm, mem_w_kv, mem_q_norm, mem_k_norm, mem_w_o,
              w_out, ffn2_norm, ffn2_w_gate, ffn2_w_up, ffn2_w_down, final_norm):
    B, S = x.shape[0], x.shape[1]
    M = mem.shape[1]
    dt = x.dtype
    pos = positions.astype(jnp.float32)[..., None]
    ret_inv = 1.0 / (RET_THETA_BASE ** jnp.linspace(0.0, 1.0, RET_DK // 2, dtype=jnp.float32))
    ret_ang = pos * ret_inv
    r_cos = jnp.cos(ret_ang)[:, :, None, :].astype(dt)
    r_sin = jnp.sin(ret_ang)[:, :, None, :].astype(dt)
    rope_inv = 1.0 / (ROPE_THETA ** (jnp.arange(0, ROT_DIM, 2, dtype=jnp.float32) / ROT_DIM))
    d_ang = pos * rope_inv
    d_cos = jnp.cos(d_ang)[:, :, None, None, :].astype(dt)
    d_sin = jnp.sin(d_ang)[:, :, None, None, :].astype(dt)

    for l in range(DEPTH):
        x = x + 0.5 * _swiglu(_rms(x, ffn1_norm[l]), ffn1_w_gate[l], ffn1_w_up[l], ffn1_w_down[l])

        h = _rms(x, mix_norm[l])
        rq, rk, rv, rg, dq, dk, dv, mq, gates = jnp.split(h @ w_in[l], IN_OFFSETS, axis=-1)

        rq = _rot_interleaved(rq.reshape(B, S, RET_HEADS, RET_DK), r_cos, r_sin)
        rk = _rot_interleaved(rk.reshape(B, S, RET_HEADS, RET_DK), r_cos, r_sin) * (RET_DK ** -0.5)
        ro = _retention(rq, rk, rv.reshape(B, S, RET_HEADS, RET_DV))
        ro = _rms(ro).reshape(B, S, RET_V_W) * jax.nn.silu(rg)
        ret_out = ro @ ret_w_o[l]

        dq = _rope_partial(_rms(dq.reshape(B, S, DIFF_HEADS, 2, DIFF_DK), diff_q_norm[l]), d_cos, d_sin)
        dk = _rope_partial(_rms(dk.reshape(B, S, DIFF_HEADS, 2, DIFF_DK), diff_k_norm[l]), d_cos, d_sin)
        lam_init = 0.8 - 0.6 * math.exp(-0.3 * l)
        lam = (jnp.exp(jnp.sum(diff_lambda_q1[l].astype(jnp.float32) * diff_lambda_k1[l].astype(jnp.float32)))
               - jnp.exp(jnp.sum(diff_lambda_q2[l].astype(jnp.float32) * diff_lambda_k2[l].astype(jnp.float32)))
               + lam_init)
        do = _diff_attention(dq, dk, dv.reshape(B, S, DIFF_HEADS, DIFF_DV), lam)
        do = _rms(do, diff_subln[l]) * (1.0 - lam_init)
        diff_out = do.reshape(B, S, DIFF_V_W) @ diff_w_o[l]

        mk, mv = jnp.split(_rms(mem, mem_norm[l]) @ mem_w_kv[l], 2, axis=-1)
        mq = _rms(mq.reshape(B, S, MEM_HEADS, MEM_DH), mem_q_norm[l])
        mk = _rms(mk.reshape(B, M, MEM_HEADS, MEM_DH), mem_k_norm[l])
        mo = _mem_attention(mq, mk, mv.reshape(B, M, MEM_HEADS, MEM_DH))
        mem_out = mo.reshape(B, S, MEM_Q_W) @ mem_w_o[l]

        g = jax.nn.sigmoid(gates + b_gate[l]).reshape(B, S, N_BRANCHES, D_MODEL)
        merged = g[:, :, 0] * ret_out + g[:, :, 1] * diff_out + g[:, :, 2] * mem_out
        x = x + merged @ w_out[l]

        x = x + 0.5 * _swiglu(_rms(x, ffn2_norm[l]), ffn2_w_gate[l], ffn2_w_up[l], ffn2_w_down[l])
        x = _rms(x, final_norm[l])
    return x
```

```python
import functools
import math

import jax
import jax.numpy as jnp
from jax import lax
from jax.experimental import pallas as pl
from jax.experimental.pallas import tpu as pltpu

D_MODEL = 1024
CHUNK = 64
EPS = 1e-6
RET_HEADS = 4
RET_DK = 256
RET_DV = 512
RET_THETA_BASE = 10000.0
DIFF_HEADS = 8
DIFF_DK = 64
DIFF_DV = 128
ROPE_THETA = 500000.0
ROT_DIM = 16
MEM_HEADS = 4
MEM_DH = 256
D_FF = 2816
LAM_INIT = 0.8 - 0.6 * math.exp(-0.3 * 0)

RET_QK_W = RET_HEADS * RET_DK
RET_V_W = RET_HEADS * RET_DV
DIFF_QK_W = DIFF_HEADS * 2 * DIFF_DK
DIFF_V_W = DIFF_HEADS * DIFF_DV
MEM_Q_W = MEM_HEADS * MEM_DH
QKV_W = 2 * RET_QK_W + 2 * RET_V_W + 2 * DIFF_QK_W + DIFF_V_W + MEM_Q_W
GATE_W = 3 * D_MODEL

LANES = 128
RET_BLOCK = 256
DIFF_QBLOCK = 256
VMEM_LIMIT = 56 * 1024 * 1024

BF16 = jnp.bfloat16
F32 = jnp.float32


def _dot(a, b):
    return jnp.dot(a, b, preferred_element_type=F32)


def _dot_nt(a, b):
    return lax.dot_general(a, b, (((1,), (1,)), ((), ())), preferred_element_type=F32)


def _dot_tn(a, b):
    return lax.dot_general(a, b, (((0,), (0,)), ((), ())), preferred_element_type=F32)


def _rms(x, w=None):
    y = x * lax.rsqrt(jnp.mean(x * x, axis=-1, keepdims=True) + EPS)
    if w is not None:
        y = y * w
    return y


def _resident(shape):
    nd = len(shape)
    return pl.BlockSpec(shape, lambda *_: (0,) * nd, pipeline_mode=pl.Buffered(1))


def _params(sem):
    return pltpu.CompilerParams(dimension_semantics=sem, vmem_limit_bytes=VMEM_LIMIT)


def _ffn_body(x_ref, nw_ref, wg_ref, wu_ref, wd_ref, fw_ref, o_ref, *, final_norm):
    x = x_ref[...]
    h = _rms(x, nw_ref[...]).astype(BF16)
    g = _dot(h, wg_ref[...])
    u = _dot(h, wu_ref[...])
    a = (g * jax.nn.sigmoid(g) * u).astype(BF16)
    y = x + 0.5 * _dot(a, wd_ref[...])
    if final_norm:
        y = _rms(y, fw_ref[...])
    o_ref[...] = y


def _ffn(x, nw, wg, wu, wd, fw, *, final_norm, tm=256):
    t, d = x.shape
    row = pl.BlockSpec((tm, d), lambda i: (i, 0))
    return pl.pallas_call(
        functools.partial(_ffn_body, final_norm=final_norm),
        out_shape=jax.ShapeDtypeStruct((t, d), F32),
        grid=(t // tm,),
        in_specs=[row, _resident(nw.shape), _resident(wg.shape), _resident(wu.shape),
                  _resident(wd.shape), _resident(fw.shape)],
        out_specs=row,
        compiler_params=_params(("parallel",)),
        name="ffn_final" if final_norm else "ffn",
    )(x, nw, wg, wu, wd, fw)


def _proj_body(x_ref, nw_ref, w_ref, rot_ref, dqn_ref, dkn_ref, mqn_ref,
               rq_ref, rk_ref, rv_ref, sg_ref, dq_ref, dk_ref, dv_ref, mq_ref):
    h = _rms(x_ref[...], nw_ref[...]).astype(BF16)
    rcos = rot_ref[:, 0:LANES]
    rsin = rot_ref[:, LANES:2 * LANES]
    dcos = rot_ref[:, 2 * LANES:3 * LANES]
    dsin = rot_ref[:, 3 * LANES:4 * LANES]
    lane = lax.broadcasted_iota(jnp.int32, (1, LANES), 1)

    off = 0
    for out_ref, scale in ((rq_ref, 1.0), (rk_ref, RET_DK ** -0.5)):
        y = _dot(h, w_ref[:, off:off + RET_QK_W])
        for hd in range(RET_HEADS):
            ye = y[:, hd * RET_DK:hd * RET_DK + LANES]
            yo = y[:, hd * RET_DK + LANES:(hd + 1) * RET_DK]
            out_ref[:, hd * RET_DK:hd * RET_DK + LANES] = ((ye * rcos - yo * rsin) * scale).astype(BF16)
            out_ref[:, hd * RET_DK + LANES:(hd + 1) * RET_DK] = ((yo * rcos + ye * rsin) * scale).astype(BF16)
        off += RET_QK_W

    rv_ref[...] = _dot(h, w_ref[:, off:off + RET_V_W]).astype(BF16)
    off += RET_V_W
    g = _dot(h, w_ref[:, off:off + RET_V_W])
    sg_ref[...] = (g * jax.nn.sigmoid(g)).astype(BF16)
    off += RET_V_W

    low = lane < DIFF_DK
    first = (lane % DIFF_DK) < (ROT_DIM // 2)
    for out_ref, n_ref, scale in ((dq_ref, dqn_ref, DIFF_DK ** -0.5), (dk_ref, dkn_ref, 1.0)):
        y = _dot(h, w_ref[:, off:off + DIFF_QK_W])
        gain = n_ref[...]
        for hd in range(DIFF_HEADS):
            yh = y[:, hd * LANES:(hd + 1) * LANES]
            sq = yh * yh
            ss_lo = jnp.sum(jnp.where(low, sq, 0.0), axis=-1, keepdims=True)
            ss_hi = jnp.sum(jnp.where(low, 0.0, sq), axis=-1, keepdims=True)
            r = jnp.where(low, lax.rsqrt(ss_lo * (1.0 / DIFF_DK) + EPS),
                          lax.rsqrt(ss_hi * (1.0 / DIFF_DK) + EPS))
            yn = yh * r * gain
            partner = jnp.where(first, pltpu.roll(yn, LANES - ROT_DIM // 2, 1),
                                pltpu.roll(yn, ROT_DIM // 2, 1))
            out_ref[:, hd * LANES:(hd + 1) * LANES] = ((yn * dcos + partner * dsin) * scale).astype(BF16)
        off += DIFF_QK_W

    dv_ref[...] = _dot(h, w_ref[:, off:off + DIFF_V_W]).astype(BF16)
    off += DIFF_V_W

    y = _dot(h, w_ref[:, off:off + MEM_Q_W])
    gain = mqn_ref[...]
    for hd in range(MEM_HEADS):
        yh = y[:, hd * MEM_DH:(hd + 1) * MEM_DH]
        mq_ref[:, hd * MEM_DH:(hd + 1) * MEM_DH] = (_rms(yh) * gain).astype(BF16)


def _proj(x1, nw, w_qkv, rot, dqn, dkn, mqn, *, tm=256):
    t, d = x1.shape
    widths = (RET_QK_W, RET_QK_W, RET_V_W, RET_V_W, DIFF_QK_W, DIFF_QK_W, DIFF_V_W, MEM_Q_W)

    def row(w):
        return pl.BlockSpec((tm, w), lambda i: (i, 0))

    return pl.pallas_call(
        _proj_body,
        out_shape=tuple(jax.ShapeDtypeStruct((t, w), BF16) for w in widths),
        grid=(t // tm,),
        in_specs=[row(d), _resident(nw.shape), _resident(w_qkv.shape), row(4 * LANES),
                  _resident(dqn.shape), _resident(dkn.shape), _resident(mqn.shape)],
        out_specs=tuple(row(w) for w in widths),
        compiler_params=_params(("parallel",)),
        name="proj",
    )(x1, nw, w_qkv, rot, dqn, dkn, mqn)


def _memkv_body(m_ref, nw_ref, w_ref, kn_ref, mk_ref, mv_ref):
    hm = _rms(m_ref[...], nw_ref[...]).astype(BF16)
    k = _dot(hm, w_ref[:, 0:MEM_Q_W])
    gain = kn_ref[...] * (MEM_DH ** -0.5)
    for hd in range(MEM_HEADS):
        kh = k[:, hd * MEM_DH:(hd + 1) * MEM_DH]
        mk_ref[:, hd * MEM_DH:(hd + 1) * MEM_DH] = (_rms(kh) * gain).astype(BF16)
    mv_ref[...] = _dot(hm, w_ref[:, MEM_Q_W:2 * MEM_Q_W]).astype(BF16)


def _memkv(mem2d, nw, w_kv, kn, *, tm=256):
    t, d = mem2d.shape
    row = pl.BlockSpec((tm, d), lambda i: (i, 0))
    out = pl.BlockSpec((tm, MEM_Q_W), lambda i: (i, 0))
    return pl.pallas_call(
        _memkv_body,
        out_shape=(jax.ShapeDtypeStruct((t, MEM_Q_W), BF16),) * 2,
        grid=(t // tm,),
        in_specs=[row, _resident(nw.shape), _resident(w_kv.shape), _resident(kn.shape)],
        out_specs=(out, out),
        compiler_params=_params(("parallel",)),
        name="memkv",
    )(mem2d, nw, w_kv, kn)


def _ret_body(q_ref, k_ref, v_ref, sg_ref, dmat_ref, qdec_ref, kdec_ref, cdec_ref, o_ref, state_ref, *, seq):
    dmat = dmat_ref[0]
    qdec = qdec_ref[0]
    kdec = kdec_ref[0]
    cdec = cdec_ref[0]
    for n in range(seq // RET_BLOCK):
        rows = slice(n * RET_BLOCK, (n + 1) * RET_BLOCK)
        q = q_ref[rows, :]
        k = k_ref[rows, :]
        v = v_ref[rows, :]
        s = _dot_nt(q, k) * dmat
        o = _dot(s.astype(BF16), v)
        kd = (k.astype(F32) * kdec).astype(BF16)
        upd = _dot_tn(kd, v)
        if n == 0:
            state_ref[...] = upd
        else:
            st = state_ref[...]
            qd = (q.astype(F32) * qdec).astype(BF16)
            o = o + _dot(qd, st.astype(BF16))
            state_ref[...] = st * cdec + upd
        o_ref[rows, :] = (_rms(o) * sg_ref[rows, :].astype(F32)).astype(BF16)


def _retention(rq, rk, rv, sg, dmat, qdec, kdec, cdec, *, batch, seq):
    qk = pl.BlockSpec((seq, RET_DK), lambda b, h: (b, h))
    vv = pl.BlockSpec((seq, RET_DV), lambda b, h: (b, h))

    def per_head(shape):
        return pl.BlockSpec((1,) + shape[1:], lambda b, h: (h, 0, 0))

    return pl.pallas_call(
        functools.partial(_ret_body, seq=seq),
        out_shape=jax.ShapeDtypeStruct(rv.shape, BF16),
        grid=(batch, RET_HEADS),
        in_specs=[qk, qk, vv, vv, per_head(dmat.shape), per_head(qdec.shape), per_head(kdec.shape),
                  per_head(cdec.shape)],
        out_specs=vv,
        scratch_shapes=[pltpu.VMEM((RET_DK, RET_DV), F32)],
        compiler_params=_params(("parallel", "parallel")),
        name="retention",
    )(rq, rk, rv, sg, dmat, qdec, kdec, cdec)


def _diff_body(q_ref, k_ref, v_ref, lamv_ref, subln_ref, o_ref, *, seq):
    lv = lamv_ref[...]
    lam = (jnp.exp(jnp.sum(lv[0:1] * lv[1:2], axis=-1, keepdims=True))
           - jnp.exp(jnp.sum(lv[2:3] * lv[3:4], axis=-1, keepdims=True)) + LAM_INIT)
    gain = subln_ref[...] * (1.0 - LAM_INIT)
    tq = DIFF_QBLOCK
    low = lax.broadcasted_iota(jnp.int32, (1, LANES), 1) < DIFF_DK
    rchunk = lax.broadcasted_iota(jnp.int32, (tq, tq), 0) // CHUNK
    cchunk = lax.broadcasted_iota(jnp.int32, (tq, tq), 1) // CHUNK
    visible = cchunk <= rchunk
    zero = jnp.zeros((), BF16)

    for qi in range(seq // tq):
        rows = slice(qi * tq, (qi + 1) * tq)
        past = qi * tq
        q = q_ref[rows, :]
        qs = (jnp.where(low, q, zero), jnp.where(low, zero, q))
        kd = k_ref[rows, :]
        sd = [jnp.where(visible, _dot_nt(qm, kd), -jnp.inf) for qm in qs]
        m = [jnp.max(s, axis=-1, keepdims=True) for s in sd]
        if past:
            kp = k_ref[0:past, :]
            sp = [_dot_nt(qm, kp) for qm in qs]
            m = [jnp.maximum(mm, jnp.max(s, axis=-1, keepdims=True)) for mm, s in zip(m, sp)]
        pd = [jnp.exp(s - mm) for s, mm in zip(sd, m)]
        l = [jnp.sum(p, axis=-1, keepdims=True) for p in pd]
        if past:
            pp = [jnp.exp(s - mm) for s, mm in zip(sp, m)]
            l = [ll + jnp.sum(p, axis=-1, keepdims=True) for ll, p in zip(l, pp)]
        c0 = 1.0 / l[0]
        c1 = lam / l[1]
        o = _dot((pd[0] * c0 - pd[1] * c1).astype(BF16), v_ref[rows, :])
        if past:
            o = o + _dot((pp[0] * c0 - pp[1] * c1).astype(BF16), v_ref[0:past, :])
        o_ref[rows, :] = (_rms(o) * gain).astype(BF16)


def _diff_attention(dq, dk, dv, lamv, subln, *, batch, seq):
    blk = pl.BlockSpec((seq, LANES), lambda b, h: (b, h))
    return pl.pallas_call(
        functools.partial(_diff_body, seq=seq),
        out_shape=jax.ShapeDtypeStruct(dv.shape, BF16),
        grid=(batch, DIFF_HEADS),
        in_specs=[blk, blk, blk, _resident(lamv.shape), _resident(subln.shape)],
        out_specs=blk,
        compiler_params=_params(("parallel", "parallel")),
        name="diff_attention",
    )(dq, dk, dv, lamv, subln)


def _merge_body(x_ref, ro_ref, do_ref, mq_ref, mk_ref, mv_ref, nw_ref, wg_ref, bg_ref,
                wro_ref, wdo_ref, wmo_ref, wout_ref, o_ref):
    x = x_ref[...]
    h = _rms(x, nw_ref[...]).astype(BF16)
    g = jax.nn.sigmoid(_dot(h, wg_ref[...]) + bg_ref[...])
    ret_out = _dot(ro_ref[...], wro_ref[...])
    diff_out = _dot(do_ref[...], wdo_ref[...])
    mem_out = None
    for hd in range(MEM_HEADS):
        cols = slice(hd * MEM_DH, (hd + 1) * MEM_DH)
        s = _dot_nt(mq_ref[:, cols], mk_ref[:, cols])
        p = jnp.exp(s - jnp.max(s, axis=-1, keepdims=True))
        p = p * (1.0 / jnp.sum(p, axis=-1, keepdims=True))
        oh = _dot(p.astype(BF16), mv_ref[:, cols])
        part = _dot(oh.astype(BF16), wmo_ref[cols, :])
        mem_out = part if mem_out is None else mem_out + part
    merged = (g[:, 0:D_MODEL] * ret_out + g[:, D_MODEL:2 * D_MODEL] * diff_out
              + g[:, 2 * D_MODEL:3 * D_MODEL] * mem_out)
    o_ref[...] = x + _dot(merged.astype(BF16), wout_ref[...])


def _merge(x1, ro, do, mq, mk, mv, nw, wg, bg, wro, wdo, wmo, wout, *, seq, mem_len, tm=256):
    t, d = x1.shape
    per_b = seq // tm

    def row(w):
        return pl.BlockSpec((tm, w), lambda i: (i, 0))

    memb = pl.BlockSpec((mem_len, MEM_Q_W), lambda i: (i // per_b, 0))
    return pl.pallas_call(
        _merge_body,
        out_shape=jax.ShapeDtypeStruct((t, d), F32),
        grid=(t // tm,),
        in_specs=[row(d), row(RET_V_W), row(DIFF_V_W), row(MEM_Q_W), memb, memb,
                  _resident(nw.shape), _resident(wg.shape), _resident(bg.shape), _resident(wro.shape),
                  _resident(wdo.shape), _resident(wmo.shape), _resident(wout.shape)],
        out_specs=row(d),
        compiler_params=_params(("parallel",)),
        name="merge",
    )(x1, ro, do, mq, mk, mv, nw, wg, bg, wro, wdo, wmo, wout)


def _deinterleave_heads(w):
    d = w.shape[0]
    return w.reshape(d, RET_HEADS, RET_DK // 2, 2).transpose(0, 1, 3, 2).reshape(d, RET_QK_W)


def _rotation_tables(positions):
    pos = positions.astype(F32).reshape(-1, 1)
    ret_inv = 1.0 / (RET_THETA_BASE ** jnp.linspace(0.0, 1.0, RET_DK // 2, dtype=F32))
    ret_ang = pos * ret_inv
    rope_inv = 1.0 / (ROPE_THETA ** (jnp.arange(0, ROT_DIM, 2, dtype=F32) / ROT_DIM))
    d_ang = pos * rope_inv
    dc, ds = jnp.cos(d_ang), jnp.sin(d_ang)
    ones = jnp.ones((pos.shape[0], DIFF_DK - ROT_DIM), F32)
    dcos = jnp.concatenate([dc, dc, ones], axis=-1)
    dsin = jnp.concatenate([-ds, ds, 0.0 * ones], axis=-1)
    return jnp.concatenate([jnp.cos(ret_ang), jnp.sin(ret_ang), dcos, dcos, dsin, dsin], axis=-1)


def _decay_tables():
    log_g = jnp.log(1.0 - 2.0 ** (-5.0 - jnp.arange(RET_HEADS, dtype=F32)))
    idx = jnp.arange(RET_BLOCK, dtype=F32)
    dist = jnp.abs(idx[:, None] - idx[None, :])
    chunk = jnp.arange(RET_BLOCK) // CHUNK
    visible = chunk[None, :] <= chunk[:, None]
    dmat = jnp.where(visible[None], jnp.exp(log_g[:, None, None] * dist[None]), 0.0)
    qdec = jnp.exp(log_g[:, None] * (idx[None, :] + 1.0))[..., None]
    kdec = jnp.exp(log_g[:, None] * (RET_BLOCK - 1.0 - idx[None, :]))[..., None]
    cdec = jnp.broadcast_to(jnp.exp(log_g * RET_BLOCK)[:, None, None], (RET_HEADS, 1, RET_DV))
    return dmat, qdec, kdec, cdec


def kernel(x, mem, positions, ffn1_norm, ffn1_w_gate, ffn1_w_up, ffn1_w_down, mix_norm, w_in, b_gate, ret_w_o, diff_q_norm, diff_k_norm, diff_lambda_q1, diff_lambda_k1, diff_lambda_q2, diff_lambda_k2, diff_subln, diff_w_o, mem_norm, mem_w_kv, mem_q_norm, mem_k_norm, mem_w_o, w_out, ffn2_norm, ffn2_w_gate, ffn2_w_up, ffn2_w_down, final_norm):
    batch, seq, d = x.shape
    mem_len = mem.shape[1]
    assert d == D_MODEL and seq % RET_BLOCK == 0 and seq % DIFF_QBLOCK == 0
    assert w_in.shape[0] == 1, "single-layer trunk"
    bf = lambda a: a.astype(BF16)
    vec = lambda a: a.reshape(1, -1)

    wi = w_in[0]
    w_qkv = bf(jnp.concatenate([_deinterleave_heads(wi[:, 0:RET_QK_W]),
                                _deinterleave_heads(wi[:, RET_QK_W:2 * RET_QK_W]),
                                wi[:, 2 * RET_QK_W:QKV_W]], axis=1))
    w_gate = bf(wi[:, QKV_W:])
    rot = _rotation_tables(positions)
    dmat, qdec, kdec, cdec = _decay_tables()
    dqn = jnp.tile(vec(diff_q_norm[0]), (1, 2))
    dkn = jnp.tile(vec(diff_k_norm[0]), (1, 2))
    lamv = jnp.stack([diff_lambda_q1[0], diff_lambda_k1[0], diff_lambda_q2[0], diff_lambda_k2[0]]).astype(F32)

    xf = x.reshape(batch * seq, d)
    x1 = _ffn(xf, vec(ffn1_norm[0]), bf(ffn1_w_gate[0]), bf(ffn1_w_up[0]), bf(ffn1_w_down[0]),
              vec(final_norm[0]), final_norm=False)
    rq, rk, rv, sg, dq, dk, dv, mq = _proj(x1, vec(mix_norm[0]), w_qkv, rot, dqn, dkn, vec(mem_q_norm[0]))
    mk, mv = _memkv(mem.reshape(batch * mem_len, d), vec(mem_norm[0]), bf(mem_w_kv[0]), vec(mem_k_norm[0]))
    ro = _retention(rq, rk, rv, sg, dmat, qdec, kdec, cdec, batch=batch, seq=seq)
    do = _diff_attention(dq, dk, dv, lamv, vec(diff_subln[0]), batch=batch, seq=seq)
    x2 = _merge(x1, ro, do, mq, mk, mv, vec(mix_norm[0]), w_gate, vec(b_gate[0]), bf(ret_w_o[0]),
                bf(diff_w_o[0]), bf(mem_w_o[0]), bf(w_out[0]), seq=seq, mem_len=mem_len)
    x3 = _ffn(x2, vec(ffn2_norm[0]), bf(ffn2_w_gate[0]), bf(ffn2_w_up[0]), bf(ffn2_w_down[0]),
              vec(final_norm[0]), final_norm=True)
    return x3.reshape(batch, seq, d)
```

```python
import functools
import math

import jax
import jax.numpy as jnp
from jax import lax
from jax.experimental import pallas as pl
from jax.experimental.pallas import tpu as pltpu

D_MODEL = 1024
CHUNK = 64
EPS = 1e-6
RET_HEADS = 4
RET_DK = 256
RET_DV = 512
RET_THETA_BASE = 10000.0
DIFF_HEADS = 8
DIFF_DK = 64
DIFF_DV = 128
ROPE_THETA = 500000.0
ROT_DIM = 16
MEM_HEADS = 4
MEM_DH = 256
D_FF = 2816
LAM_INIT = 0.8 - 0.6 * math.exp(-0.3 * 0)

RET_QK_W = RET_HEADS * RET_DK
RET_V_W = RET_HEADS * RET_DV
DIFF_QK_W = DIFF_HEADS * 2 * DIFF_DK
DIFF_V_W = DIFF_HEADS * DIFF_DV
MEM_Q_W = MEM_HEADS * MEM_DH
QKV_W = 2 * RET_QK_W + 2 * RET_V_W + 2 * DIFF_QK_W + DIFF_V_W + MEM_Q_W
GATE_W = 3 * D_MODEL

LANES = 128
RET_BLOCK = 256
DIFF_QBLOCK = 256
VMEM_LIMIT = 56 * 1024 * 1024

BF16 = jnp.bfloat16
F32 = jnp.float32


def _dot(a, b):
    return jnp.dot(a, b, preferred_element_type=F32)


def _dot_nt(a, b):
    return lax.dot_general(a, b, (((1,), (1,)), ((), ())), preferred_element_type=F32)


def _dot_tn(a, b):
    return lax.dot_general(a, b, (((0,), (0,)), ((), ())), preferred_element_type=F32)


def _rms(x, w=None):
    y = x * lax.rsqrt(jnp.mean(x * x, axis=-1, keepdims=True) + EPS)
    if w is not None:
        y = y * w
    return y


def _resident(shape):
    nd = len(shape)
    return pl.BlockSpec(shape, lambda *_: (0,) * nd, pipeline_mode=pl.Buffered(1))


def _resident_cols(rows, width, index):
    return pl.BlockSpec((rows, width), lambda *_: (0, index), pipeline_mode=pl.Buffered(1))


def _params(sem):
    return pltpu.CompilerParams(dimension_semantics=sem, vmem_limit_bytes=VMEM_LIMIT)


def _ffn_body(x_ref, nw_ref, wg_ref, wu_ref, wd_ref, fw_ref, o_ref, *, final_norm):
    x = x_ref[...]
    h = _rms(x, nw_ref[...]).astype(BF16)
    g = _dot(h, wg_ref[...])
    u = _dot(h, wu_ref[...])
    a = (g * jax.nn.sigmoid(g) * u).astype(BF16)
    y = x + 0.5 * _dot(a, wd_ref[...])
    if final_norm:
        y = _rms(y, fw_ref[...])
    o_ref[...] = y


def _ffn(x, nw, wg, wu, wd, fw, *, final_norm, tm=512):
    t, d = x.shape
    row = pl.BlockSpec((tm, d), lambda i: (i, 0))
    return pl.pallas_call(
        functools.partial(_ffn_body, final_norm=final_norm),
        out_shape=jax.ShapeDtypeStruct((t, d), F32),
        grid=(t // tm,),
        in_specs=[row, _resident(nw.shape), _resident(wg.shape), _resident(wu.shape),
                  _resident(wd.shape), _resident(fw.shape)],
        out_specs=row,
        compiler_params=_params(("parallel",)),
        name="ffn_final" if final_norm else "ffn",
    )(x, nw, wg, wu, wd, fw)


def _proj_body(x_ref, nw_ref, wrqk_ref, wrv_ref, wrg_ref, wdq_ref, wdk_ref, wdv_ref, wmq_ref,
               rcos_ref, rsin_ref, dcos_ref, dsin_ref, dqn_ref, dkn_ref, mqn_ref,
               rq_ref, rk_ref, rv_ref, sg_ref, dq_ref, dk_ref, dv_ref, mq_ref):
    h = _rms(x_ref[...], nw_ref[...]).astype(BF16)
    rcos = rcos_ref[...]
    rsin = rsin_ref[...]
    dcos = dcos_ref[...]
    dsin = dsin_ref[...]
    lane = lax.broadcasted_iota(jnp.int32, (1, LANES), 1)

    off = 0
    for out_ref, scale in ((rq_ref, 1.0), (rk_ref, RET_DK ** -0.5)):
        y = _dot(h, wrqk_ref[:, off:off + RET_QK_W])
        for hd in range(RET_HEADS):
            ye = y[:, hd * RET_DK:hd * RET_DK + LANES]
            yo = y[:, hd * RET_DK + LANES:(hd + 1) * RET_DK]
            out_ref[:, hd * RET_DK:hd * RET_DK + LANES] = ((ye * rcos - yo * rsin) * scale).astype(BF16)
            out_ref[:, hd * RET_DK + LANES:(hd + 1) * RET_DK] = ((yo * rcos + ye * rsin) * scale).astype(BF16)
        off += RET_QK_W

    rv_ref[...] = _dot(h, wrv_ref[...]).astype(BF16)
    g = _dot(h, wrg_ref[...])
    sg_ref[...] = (g * jax.nn.sigmoid(g)).astype(BF16)

    low = lane < DIFF_DK
    first = (lane % DIFF_DK) < (ROT_DIM // 2)
    for out_ref, w_ref, n_ref, scale in ((dq_ref, wdq_ref, dqn_ref, DIFF_DK ** -0.5),
                                         (dk_ref, wdk_ref, dkn_ref, 1.0)):
        y = _dot(h, w_ref[...])
        gain = n_ref[...]
        for hd in range(DIFF_HEADS):
            yh = y[:, hd * LANES:(hd + 1) * LANES]
            sq = yh * yh
            ss_lo = jnp.sum(jnp.where(low, sq, 0.0), axis=-1, keepdims=True)
            ss_hi = jnp.sum(jnp.where(low, 0.0, sq), axis=-1, keepdims=True)
            r = jnp.where(low, lax.rsqrt(ss_lo * (1.0 / DIFF_DK) + EPS),
                          lax.rsqrt(ss_hi * (1.0 / DIFF_DK) + EPS))
            yn = yh * r * gain
            partner = jnp.where(first, pltpu.roll(yn, LANES - ROT_DIM // 2, 1),
                                pltpu.roll(yn, ROT_DIM // 2, 1))
            out_ref[:, hd * LANES:(hd + 1) * LANES] = ((yn * dcos + partner * dsin) * scale).astype(BF16)

    dv_ref[...] = _dot(h, wdv_ref[...]).astype(BF16)

    y = _dot(h, wmq_ref[...])
    gain = mqn_ref[...]
    for hd in range(MEM_HEADS):
        yh = y[:, hd * MEM_DH:(hd + 1) * MEM_DH]
        mq_ref[:, hd * MEM_DH:(hd + 1) * MEM_DH] = (_rms(yh) * gain).astype(BF16)


def _proj(x1, nw, w_rqk, w_in, rcos, rsin, dcos, dsin, dqn, dkn, mqn, *, tm=256):
    t, d = x1.shape
    widths = (RET_QK_W, RET_QK_W, RET_V_W, RET_V_W, DIFF_QK_W, DIFF_QK_W, DIFF_V_W, MEM_Q_W)

    def row(w):
        return pl.BlockSpec((tm, w), lambda i: (i, 0))

    return pl.pallas_call(
        _proj_body,
        out_shape=tuple(jax.ShapeDtypeStruct((t, w), BF16) for w in widths),
        grid=(t // tm,),
        in_specs=[row(d), _resident(nw.shape), _resident(w_rqk.shape),
                  _resident_cols(d, RET_V_W, 1), _resident_cols(d, RET_V_W, 2),
                  _resident_cols(d, DIFF_QK_W, 6), _resident_cols(d, DIFF_QK_W, 7),
                  _resident_cols(d, DIFF_V_W, 8), _resident_cols(d, MEM_Q_W, 9),
                  row(LANES), row(LANES), row(LANES), row(LANES),
                  _resident(dqn.shape), _resident(dkn.shape), _resident(mqn.shape)],
        out_specs=tuple(row(w) for w in widths),
        compiler_params=_params(("parallel",)),
        name="proj",
    )(x1, nw, w_rqk, w_in, w_in, w_in, w_in, w_in, w_in, rcos, rsin, dcos, dsin, dqn, dkn, mqn)


def _memkv_body(m_ref, nw_ref, w_ref, kn_ref, mk_ref, mv_ref):
    hm = _rms(m_ref[...], nw_ref[...]).astype(BF16)
    k = _dot(hm, w_ref[:, 0:MEM_Q_W])
    gain = kn_ref[...] * (MEM_DH ** -0.5)
    for hd in range(MEM_HEADS):
        kh = k[:, hd * MEM_DH:(hd + 1) * MEM_DH]
        mk_ref[:, hd * MEM_DH:(hd + 1) * MEM_DH] = (_rms(kh) * gain).astype(BF16)
    mv_ref[...] = _dot(hm, w_ref[:, MEM_Q_W:2 * MEM_Q_W]).astype(BF16)


def _memkv(mem2d, nw, w_kv, kn, *, tm=256):
    t, d = mem2d.shape
    row = pl.BlockSpec((tm, d), lambda i: (i, 0))
    out = pl.BlockSpec((tm, MEM_Q_W), lambda i: (i, 0))
    return pl.pallas_call(
        _memkv_body,
        out_shape=(jax.ShapeDtypeStruct((t, MEM_Q_W), BF16),) * 2,
        grid=(t // tm,),
        in_specs=[row, _resident(nw.shape), _resident(w_kv.shape), _resident(kn.shape)],
        out_specs=(out, out),
        compiler_params=_params(("parallel",)),
        name="memkv",
    )(mem2d, nw, w_kv, kn)


def _ret_body(q_ref, k_ref, v_ref, sg_ref, dmat_ref, qdec_ref, kdec_ref, cdec_ref, o_ref, state_ref, *, seq):
    dmat = dmat_ref[0]
    qdec = qdec_ref[0]
    kdec = kdec_ref[0]
    cdec = cdec_ref[0]
    for n in range(seq // RET_BLOCK):
        rows = slice(n * RET_BLOCK, (n + 1) * RET_BLOCK)
        q = q_ref[rows, :]
        k = k_ref[rows, :]
        v = v_ref[rows, :]
        s = _dot_nt(q, k) * dmat
        o = _dot(s.astype(BF16), v)
        kd = (k.astype(F32) * kdec).astype(BF16)
        upd = _dot_tn(kd, v)
        if n == 0:
            state_ref[...] = upd
        else:
            st = state_ref[...]
            qd = (q.astype(F32) * qdec).astype(BF16)
            o = o + _dot(qd, st.astype(BF16))
            state_ref[...] = st * cdec + upd
        o_ref[rows, :] = (_rms(o) * sg_ref[rows, :].astype(F32)).astype(BF16)


def _retention(rq, rk, rv, sg, dmat, qdec, kdec, cdec, *, batch, seq):
    qk = pl.BlockSpec((seq, RET_DK), lambda b, h: (b, h))
    vv = pl.BlockSpec((seq, RET_DV), lambda b, h: (b, h))

    def per_head(shape):
        return pl.BlockSpec((1,) + shape[1:], lambda b, h: (h, 0, 0))

    return pl.pallas_call(
        functools.partial(_ret_body, seq=seq),
        out_shape=jax.ShapeDtypeStruct(rv.shape, BF16),
        grid=(batch, RET_HEADS),
        in_specs=[qk, qk, vv, vv, per_head(dmat.shape), per_head(qdec.shape), per_head(kdec.shape),
                  per_head(cdec.shape)],
        out_specs=vv,
        scratch_shapes=[pltpu.VMEM((RET_DK, RET_DV), F32)],
        compiler_params=_params(("parallel", "parallel")),
        name="retention",
    )(rq, rk, rv, sg, dmat, qdec, kdec, cdec)


def _diff_body(q_ref, k_ref, v_ref, lamv_ref, subln_ref, o_ref, *, seq):
    lv = lamv_ref[...]
    lam = (jnp.exp(jnp.sum(lv[0:1] * lv[1:2], axis=-1, keepdims=True))
           - jnp.exp(jnp.sum(lv[2:3] * lv[3:4], axis=-1, keepdims=True)) + LAM_INIT)
    gain = subln_ref[...] * (1.0 - LAM_INIT)
    tq = DIFF_QBLOCK
    low = lax.broadcasted_iota(jnp.int32, (1, LANES), 1) < DIFF_DK
    rchunk = lax.broadcasted_iota(jnp.int32, (tq, tq), 0) // CHUNK
    cchunk = lax.broadcasted_iota(jnp.int32, (tq, tq), 1) // CHUNK
    visible = cchunk <= rchunk
    zero = jnp.zeros((), BF16)

    for qi in range(seq // tq):
        rows = slice(qi * tq, (qi + 1) * tq)
        past = qi * tq
        q = q_ref[rows, :]
        qs = (jnp.where(low, q, zero), jnp.where(low, zero, q))
        kd = k_ref[rows, :]
        sd = [jnp.where(visible, _dot_nt(qm, kd), -jnp.inf) for qm in qs]
        m = [jnp.max(s, axis=-1, keepdims=True) for s in sd]
        if past:
            kp = k_ref[0:past, :]
            sp = [_dot_nt(qm, kp) for qm in qs]
            m = [jnp.maximum(mm, jnp.max(s, axis=-1, keepdims=True)) for mm, s in zip(m, sp)]
        pd = [jnp.exp(s - mm) for s, mm in zip(sd, m)]
        l = [jnp.sum(p, axis=-1, keepdims=True) for p in pd]
        if past:
            pp = [jnp.exp(s - mm) for s, mm in zip(sp, m)]
            l = [ll + jnp.sum(p, axis=-1, keepdims=True) for ll, p in zip(l, pp)]
        c0 = 1.0 / l[0]
        c1 = lam / l[1]
        o = _dot((pd[0] * c0 - pd[1] * c1).astype(BF16), v_ref[rows, :])
        if past:
            o = o + _dot((pp[0] * c0 - pp[1] * c1).astype(BF16), v_ref[0:past, :])
        o_ref[rows, :] = (_rms(o) * gain).astype(BF16)


def _diff_attention(dq, dk, dv, lamv, subln, *, batch, seq):
    blk = pl.BlockSpec((seq, LANES), lambda b, h: (b, h))
    return pl.pallas_call(
        functools.partial(_diff_body, seq=seq),
        out_shape=jax.ShapeDtypeStruct(dv.shape, BF16),
        grid=(batch, DIFF_HEADS),
        in_specs=[blk, blk, blk, _resident(lamv.shape), _resident(subln.shape)],
        out_specs=blk,
        compiler_params=_params(("parallel", "parallel")),
        name="diff_attention",
    )(dq, dk, dv, lamv, subln)


def _merge_body(x_ref, ro_ref, do_ref, mq_ref, mk_ref, mv_ref, nw_ref, wg0_ref, wg1_ref, wg2_ref, bg_ref,
                wro_ref, wdo_ref, wmo_ref, wout_ref, o_ref):
    x = x_ref[...]
    h = _rms(x, nw_ref[...]).astype(BF16)
    g = [jax.nn.sigmoid(_dot(h, w_ref[...]) + bg_ref[:, i * D_MODEL:(i + 1) * D_MODEL])
         for i, w_ref in enumerate((wg0_ref, wg1_ref, wg2_ref))]
    ret_out = _dot(ro_ref[...], wro_ref[...])
    diff_out = _dot(do_ref[...], wdo_ref[...])
    mem_out = None
    for hd in range(MEM_HEADS):
        cols = slice(hd * MEM_DH, (hd + 1) * MEM_DH)
        s = _dot_nt(mq_ref[:, cols], mk_ref[:, cols])
        p = jnp.exp(s - jnp.max(s, axis=-1, keepdims=True))
        p = p * (1.0 / jnp.sum(p, axis=-1, keepdims=True))
        oh = _dot(p.astype(BF16), mv_ref[:, cols])
        part = _dot(oh.astype(BF16), wmo_ref[cols, :])
        mem_out = part if mem_out is None else mem_out + part
    merged = g[0] * ret_out + g[1] * diff_out + g[2] * mem_out
    o_ref[...] = x + _dot(merged.astype(BF16), wout_ref[...])


def _merge(x1, ro, do, mq, mk, mv, nw, w_in, bg, wro, wdo, wmo, wout, *, seq, mem_len, tm=256):
    t, d = x1.shape
    per_b = seq // tm

    def row(w):
        return pl.BlockSpec((tm, w), lambda i: (i, 0))

    memb = pl.BlockSpec((mem_len, MEM_Q_W), lambda i: (i // per_b, 0))
    return pl.pallas_call(
        _merge_body,
        out_shape=jax.ShapeDtypeStruct((t, d), F32),
        grid=(t // tm,),
        in_specs=[row(d), row(RET_V_W), row(DIFF_V_W), row(MEM_Q_W), memb, memb,
                  _resident(nw.shape), _resident_cols(d, D_MODEL, QKV_W // D_MODEL),
                  _resident_cols(d, D_MODEL, QKV_W // D_MODEL + 1),
                  _resident_cols(d, D_MODEL, QKV_W // D_MODEL + 2), _resident(bg.shape), _resident(wro.shape),
                  _resident(wdo.shape), _resident(wmo.shape), _resident(wout.shape)],
        out_specs=row(d),
        compiler_params=_params(("parallel",)),
        name="merge",
    )(x1, ro, do, mq, mk, mv, nw, w_in, w_in, w_in, bg, wro, wdo, wmo, wout)


def _deinterleave_heads(w):
    d = w.shape[0]
    return w.reshape(d, RET_HEADS, RET_DK // 2, 2).transpose(0, 1, 3, 2).reshape(d, RET_QK_W)


def _rotation_tables(positions):
    pos = positions.astype(F32).reshape(-1, 1)
    ret_inv = 1.0 / (RET_THETA_BASE ** jnp.linspace(0.0, 1.0, RET_DK // 2, dtype=F32))
    ret_ang = pos * ret_inv
    rope_inv = 1.0 / (ROPE_THETA ** (jnp.arange(0, ROT_DIM, 2, dtype=F32) / ROT_DIM))
    d_ang = pos * rope_inv
    dc, ds = jnp.cos(d_ang), jnp.sin(d_ang)
    ones = jnp.ones((pos.shape[0], DIFF_DK - ROT_DIM), F32)
    zeros = jnp.zeros_like(ones)
    dcos = jnp.concatenate([dc, dc, ones, dc, dc, ones], axis=-1)
    dsin = jnp.concatenate([-ds, ds, zeros, -ds, ds, zeros], axis=-1)
    return jnp.cos(ret_ang), jnp.sin(ret_ang), dcos, dsin


def _decay_tables():
    log_g = jnp.log(1.0 - 2.0 ** (-5.0 - jnp.arange(RET_HEADS, dtype=F32)))
    idx = jnp.arange(RET_BLOCK, dtype=F32)
    dist = jnp.abs(idx[:, None] - idx[None, :])
    chunk = jnp.arange(RET_BLOCK) // CHUNK
    visible = chunk[None, :] <= chunk[:, None]
    dmat = jnp.where(visible[None], jnp.exp(log_g[:, None, None] * dist[None]), 0.0)
    qdec = jnp.exp(log_g[:, None] * (idx[None, :] + 1.0))[..., None]
    kdec = jnp.exp(log_g[:, None] * (RET_BLOCK - 1.0 - idx[None, :]))[..., None]
    cdec = jnp.broadcast_to(jnp.exp(log_g * RET_BLOCK)[:, None, None], (RET_HEADS, 1, RET_DV))
    return dmat, qdec, kdec, cdec


def kernel(x, mem, positions, ffn1_norm, ffn1_w_gate, ffn1_w_up, ffn1_w_down, mix_norm, w_in, b_gate, ret_w_o, diff_q_norm, diff_k_norm, diff_lambda_q1, diff_lambda_k1, diff_lambda_q2, diff_lambda_k2, diff_subln, diff_w_o, mem_norm, mem_w_kv, mem_q_norm, mem_k_norm, mem_w_o, w_out, ffn2_norm, ffn2_w_gate, ffn2_w_up, ffn2_w_down, final_norm):
    batch, seq, d = x.shape
    mem_len = mem.shape[1]
    assert d == D_MODEL and seq % RET_BLOCK == 0 and seq % DIFF_QBLOCK == 0
    assert w_in.shape[0] == 1, "single-layer trunk"
    bf = lambda a: a.astype(BF16)
    vec = lambda a: a.reshape(1, -1)

    wi = bf(w_in[0])
    w_rqk = jnp.concatenate([_deinterleave_heads(wi[:, 0:RET_QK_W]),
                             _deinterleave_heads(wi[:, RET_QK_W:2 * RET_QK_W])], axis=1)
    rcos, rsin, dcos, dsin = _rotation_tables(positions)
    dmat, qdec, kdec, cdec = _decay_tables()
    dqn = jnp.tile(vec(diff_q_norm[0]), (1, 2))
    dkn = jnp.tile(vec(diff_k_norm[0]), (1, 2))
    lamv = jnp.stack([diff_lambda_q1[0], diff_lambda_k1[0], diff_lambda_q2[0], diff_lambda_k2[0]]).astype(F32)

    xf = x.reshape(batch * seq, d)
    x1 = _ffn(xf, vec(ffn1_norm[0]), bf(ffn1_w_gate[0]), bf(ffn1_w_up[0]), bf(ffn1_w_down[0]),
              vec(final_norm[0]), final_norm=False)
    rq, rk, rv, sg, dq, dk, dv, mq = _proj(x1, vec(mix_norm[0]), w_rqk, wi, rcos, rsin, dcos, dsin, dqn, dkn,
                                           vec(mem_q_norm[0]))
    mk, mv = _memkv(mem.reshape(batch * mem_len, d), vec(mem_norm[0]), bf(mem_w_kv[0]), vec(mem_k_norm[0]))
    ro = _retention(rq, rk, rv, sg, dmat, qdec, kdec, cdec, batch=batch, seq=seq)
    do = _diff_attention(dq, dk, dv, lamv, vec(diff_subln[0]), batch=batch, seq=seq)
    x2 = _merge(x1, ro, do, mq, mk, mv, vec(mix_norm[0]), wi, vec(b_gate[0]), bf(ret_w_o[0]),
                bf(diff_w_o[0]), bf(mem_w_o[0]), bf(w_out[0]), seq=seq, mem_len=mem_len)
    x3 = _ffn(x2, vec(ffn2_norm[0]), bf(ffn2_w_gate[0]), bf(ffn2_w_up[0]), bf(ffn2_w_down[0]),
              vec(final_norm[0]), final_norm=True)
    return x3.reshape(batch, seq, d)
```

```python
import functools
import math

import jax
import jax.numpy as jnp
from jax import lax
from jax.experimental import pallas as pl
from jax.experimental.pallas import tpu as pltpu

D_MODEL = 1024
CHUNK = 64
EPS = 1e-6
RET_HEADS = 4
RET_DK = 256
RET_DV = 512
RET_THETA_BASE = 10000.0
DIFF_HEADS = 8
DIFF_DK = 64
DIFF_DV = 128
ROPE_THETA = 500000.0
ROT_DIM = 16
MEM_HEADS = 4
MEM_DH = 256
D_FF = 2816
LAM_INIT = 0.8 - 0.6 * math.exp(-0.3 * 0)
LOG2E = math.log2(math.e)

RET_QK_W = RET_HEADS * RET_DK
RET_V_W = RET_HEADS * RET_DV
DIFF_QK_W = DIFF_HEADS * 2 * DIFF_DK
DIFF_V_W = DIFF_HEADS * DIFF_DV
MEM_Q_W = MEM_HEADS * MEM_DH
QKV_W = 2 * RET_QK_W + 2 * RET_V_W + 2 * DIFF_QK_W + DIFF_V_W + MEM_Q_W
GATE_W = 3 * D_MODEL

LANES = 128
RET_BLOCK = 256
DIFF_QBLOCK = 256
DIFF_LOOKAHEAD = 1
VMEM_LIMIT = 56 * 1024 * 1024

BF16 = jnp.bfloat16
F32 = jnp.float32


def _dot(a, b):
    return jnp.dot(a, b, preferred_element_type=F32)


def _dot_nt(a, b):
    return lax.dot_general(a, b, (((1,), (1,)), ((), ())), preferred_element_type=F32)


def _dot_tn(a, b):
    return lax.dot_general(a, b, (((0,), (0,)), ((), ())), preferred_element_type=F32)


def _rms(x, w=None):
    y = x * lax.rsqrt(jnp.mean(x * x, axis=-1, keepdims=True) + EPS)
    if w is not None:
        y = y * w
    return y


def _resident(shape):
    nd = len(shape)
    return pl.BlockSpec(shape, lambda *_: (0,) * nd, pipeline_mode=pl.Buffered(1))


def _resident_cols(rows, width, index):
    return pl.BlockSpec((rows, width), lambda *_: (0, index), pipeline_mode=pl.Buffered(1))


def _params(sem):
    return pltpu.CompilerParams(dimension_semantics=sem, vmem_limit_bytes=VMEM_LIMIT)


def _ffn_body(x_ref, nw_ref, wg_ref, wu_ref, wd_ref, fw_ref, o_ref, *, final_norm):
    x = x_ref[...]
    h = _rms(x, nw_ref[...]).astype(BF16)
    g = _dot(h, wg_ref[...])
    u = _dot(h, wu_ref[...])
    a = (g * jax.nn.sigmoid(g) * u).astype(BF16)
    y = x + 0.5 * _dot(a, wd_ref[...])
    if final_norm:
        y = _rms(y, fw_ref[...])
    o_ref[...] = y


def _ffn(x, nw, wg, wu, wd, fw, *, final_norm, tm=512):
    t, d = x.shape
    row = pl.BlockSpec((tm, d), lambda i: (i, 0))
    return pl.pallas_call(
        functools.partial(_ffn_body, final_norm=final_norm),
        out_shape=jax.ShapeDtypeStruct((t, d), F32),
        grid=(t // tm,),
        in_specs=[row, _resident(nw.shape), _resident(wg.shape), _resident(wu.shape),
                  _resident(wd.shape), _resident(fw.shape)],
        out_specs=row,
        compiler_params=_params(("parallel",)),
        name="ffn_final" if final_norm else "ffn",
    )(x, nw, wg, wu, wd, fw)


def _proj_body(x_ref, nw_ref, wrqk_ref, wrv_ref, wrg_ref, wdq_ref, wdk_ref, wdv_ref, wmq_ref,
               pos_ref, rinv_ref, dtab_ref, dqn_ref, dkn_ref, mqn_ref,
               rq_ref, rk_ref, rv_ref, sg_ref, dq_ref, dk_ref, dv_ref, mq_ref):
    h = _rms(x_ref[...], nw_ref[...]).astype(BF16)
    lane = lax.broadcasted_iota(jnp.int32, (1, LANES), 1)
    half = ROT_DIM // 2

    ret_ang = pos_ref[...] * rinv_ref[...]
    rcos = jnp.cos(ret_ang)
    rsin = jnp.sin(ret_ang)

    dtab = dtab_ref[...]
    c = jnp.where(lane < half, dtab, 0.0)
    c = c + pltpu.roll(c, half, 1)
    c = c + pltpu.roll(c, DIFF_DK, 1)
    dcos = jnp.where((lane % DIFF_DK) < ROT_DIM, c, 1.0)
    sn = jnp.where((lane >= half) & (lane < ROT_DIM), dtab, 0.0)
    sn = sn - pltpu.roll(sn, LANES - half, 1)
    dsin = sn + pltpu.roll(sn, DIFF_DK, 1)

    off = 0
    for out_ref, scale in ((rq_ref, 1.0), (rk_ref, RET_DK ** -0.5)):
        y = _dot(h, wrqk_ref[:, off:off + RET_QK_W])
        for hd in range(RET_HEADS):
            ye = y[:, hd * RET_DK:hd * RET_DK + LANES]
            yo = y[:, hd * RET_DK + LANES:(hd + 1) * RET_DK]
            out_ref[:, hd * RET_DK:hd * RET_DK + LANES] = ((ye * rcos - yo * rsin) * scale).astype(BF16)
            out_ref[:, hd * RET_DK + LANES:(hd + 1) * RET_DK] = ((yo * rcos + ye * rsin) * scale).astype(BF16)
        off += RET_QK_W

    rv_ref[...] = _dot(h, wrv_ref[...]).astype(BF16)
    g = _dot(h, wrg_ref[...])
    sg_ref[...] = (g * jax.nn.sigmoid(g)).astype(BF16)

    low = lane < DIFF_DK
    first = (lane % DIFF_DK) < half
    for out_ref, w_ref, n_ref, scale in ((dq_ref, wdq_ref, dqn_ref, LOG2E * DIFF_DK ** -0.5),
                                         (dk_ref, wdk_ref, dkn_ref, 1.0)):
        y = _dot(h, w_ref[...])
        gain = n_ref[...]
        for hd in range(DIFF_HEADS):
            yh = y[:, hd * LANES:(hd + 1) * LANES]
            sq = yh * yh
            ss_lo = jnp.sum(jnp.where(low, sq, 0.0), axis=-1, keepdims=True)
            ss_hi = jnp.sum(jnp.where(low, 0.0, sq), axis=-1, keepdims=True)
            r = jnp.where(low, lax.rsqrt(ss_lo * (1.0 / DIFF_DK) + EPS),
                          lax.rsqrt(ss_hi * (1.0 / DIFF_DK) + EPS))
            yn = yh * r * gain
            partner = jnp.where(first, pltpu.roll(yn, LANES - ROT_DIM // 2, 1),
                                pltpu.roll(yn, ROT_DIM // 2, 1))
            out_ref[:, hd * LANES:(hd + 1) * LANES] = ((yn * dcos + partner * dsin) * scale).astype(BF16)

    dv_ref[...] = _dot(h, wdv_ref[...]).astype(BF16)

    y = _dot(h, wmq_ref[...])
    gain = mqn_ref[...]
    for hd in range(MEM_HEADS):
        yh = y[:, hd * MEM_DH:(hd + 1) * MEM_DH]
        mq_ref[:, hd * MEM_DH:(hd + 1) * MEM_DH] = (_rms(yh) * gain).astype(BF16)


def _proj(x1, nw, w_rqk, w_in, pos, rinv, dtab, dqn, dkn, mqn, *, tm=256):
    t, d = x1.shape
    widths = (RET_QK_W, RET_QK_W, RET_V_W, RET_V_W, DIFF_QK_W, DIFF_QK_W, DIFF_V_W, MEM_Q_W)

    def row(w):
        return pl.BlockSpec((tm, w), lambda i: (i, 0))

    return pl.pallas_call(
        _proj_body,
        out_shape=tuple(jax.ShapeDtypeStruct((t, w), BF16) for w in widths),
        grid=(t // tm,),
        in_specs=[row(d), _resident(nw.shape), _resident(w_rqk.shape),
                  _resident_cols(d, RET_V_W, 1), _resident_cols(d, RET_V_W, 2),
                  _resident_cols(d, DIFF_QK_W, 6), _resident_cols(d, DIFF_QK_W, 7),
                  _resident_cols(d, DIFF_V_W, 8), _resident_cols(d, MEM_Q_W, 9),
                  row(1), _resident(rinv.shape), row(LANES),
                  _resident(dqn.shape), _resident(dkn.shape), _resident(mqn.shape)],
        out_specs=tuple(row(w) for w in widths),
        compiler_params=_params(("parallel",)),
        name="proj",
    )(x1, nw, w_rqk, w_in, w_in, w_in, w_in, w_in, w_in, pos, rinv, dtab, dqn, dkn, mqn)


def _memkv_body(m_ref, nw_ref, w_ref, kn_ref, mk_ref, mv_ref):
    hm = _rms(m_ref[...], nw_ref[...]).astype(BF16)
    k = _dot(hm, w_ref[:, 0:MEM_Q_W])
    gain = kn_ref[...] * (MEM_DH ** -0.5)
    for hd in range(MEM_HEADS):
        kh = k[:, hd * MEM_DH:(hd + 1) * MEM_DH]
        mk_ref[:, hd * MEM_DH:(hd + 1) * MEM_DH] = (_rms(kh) * gain).astype(BF16)
    mv_ref[...] = _dot(hm, w_ref[:, MEM_Q_W:2 * MEM_Q_W]).astype(BF16)


def _memkv(mem2d, nw, w_kv, kn, *, tm=256):
    t, d = mem2d.shape
    row = pl.BlockSpec((tm, d), lambda i: (i, 0))
    out = pl.BlockSpec((tm, MEM_Q_W), lambda i: (i, 0))
    return pl.pallas_call(
        _memkv_body,
        out_shape=(jax.ShapeDtypeStruct((t, MEM_Q_W), BF16),) * 2,
        grid=(t // tm,),
        in_specs=[row, _resident(nw.shape), _resident(w_kv.shape), _resident(kn.shape)],
        out_specs=(out, out),
        compiler_params=_params(("parallel",)),
        name="memkv",
    )(mem2d, nw, w_kv, kn)


def _ret_body(q_ref, k_ref, v_ref, sg_ref, dmat_ref, qdec_ref, kdec_ref, cdec_ref, o_ref, state_ref, *, seq):
    dmat = dmat_ref[0]
    qdec = qdec_ref[0]
    kdec = kdec_ref[0]
    cdec = cdec_ref[0]
    for n in range(seq // RET_BLOCK):
        rows = slice(n * RET_BLOCK, (n + 1) * RET_BLOCK)
        q = q_ref[rows, :]
        k = k_ref[rows, :]
        v = v_ref[rows, :]
        s = _dot_nt(q, k) * dmat
        o = _dot(s.astype(BF16), v)
        kd = (k.astype(F32) * kdec).astype(BF16)
        upd = _dot_tn(kd, v)
        if n == 0:
            state_ref[...] = upd
        else:
            st = state_ref[...]
            qd = (q.astype(F32) * qdec).astype(BF16)
            o = o + _dot(qd, st.astype(BF16))
            state_ref[...] = st * cdec + upd
        o_ref[rows, :] = (_rms(o) * sg_ref[rows, :].astype(F32)).astype(BF16)


def _retention(rq, rk, rv, sg, dmat, qdec, kdec, cdec, *, batch, seq):
    qk = pl.BlockSpec((seq, RET_DK), lambda b, h: (b, h))
    vv = pl.BlockSpec((seq, RET_DV), lambda b, h: (b, h))

    def per_head(shape):
        return pl.BlockSpec((1,) + shape[1:], lambda b, h: (h, 0, 0))

    return pl.pallas_call(
        functools.partial(_ret_body, seq=seq),
        out_shape=jax.ShapeDtypeStruct(rv.shape, BF16),
        grid=(batch, RET_HEADS),
        in_specs=[qk, qk, vv, vv, per_head(dmat.shape), per_head(qdec.shape), per_head(kdec.shape),
                  per_head(cdec.shape)],
        out_specs=vv,
        scratch_shapes=[pltpu.VMEM((RET_DK, RET_DV), F32)],
        compiler_params=_params(("parallel", "parallel")),
        name="retention",
    )(rq, rk, rv, sg, dmat, qdec, kdec, cdec)


def _diff_body(q_ref, k_ref, v_ref, lamv_ref, subln_ref, o_ref, vx_ref, *, seq):
    lv = lamv_ref[...]
    lam = (jnp.exp(jnp.sum(lv[0:1] * lv[1:2], axis=-1, keepdims=True))
           - jnp.exp(jnp.sum(lv[2:3] * lv[3:4], axis=-1, keepdims=True)) + LAM_INIT)
    gain = subln_ref[...] * (1.0 - LAM_INIT)
    tq = DIFF_QBLOCK
    low = lax.broadcasted_iota(jnp.int32, (1, LANES), 1) < DIFF_DK
    rchunk = (lax.broadcasted_iota(jnp.int32, (2 * tq, tq), 0) % tq) // CHUNK
    cchunk = lax.broadcasted_iota(jnp.int32, (2 * tq, tq), 1) // CHUNK
    visible = cchunk <= rchunk
    zero = jnp.zeros((), BF16)

    vx_ref[:, 0:DIFF_DV] = v_ref[...]
    vx_ref[:, DIFF_DV:2 * DIFF_DV] = jnp.ones((seq, DIFF_DV), BF16)

    def scores(qi):
        rows = slice(qi * tq, (qi + 1) * tq)
        q = q_ref[rows, :]
        q2 = jnp.concatenate([jnp.where(low, q, zero), jnp.where(low, zero, q)], axis=0)
        sd = jnp.where(visible, _dot_nt(q2, k_ref[rows, :]), -jnp.inf)
        sp = _dot_nt(q2, k_ref[0:qi * tq, :]) if qi else None
        return sd, sp

    nq = seq // tq
    order = list(reversed(range(nq)))
    ahead = [scores(qi) for qi in order[:DIFF_LOOKAHEAD]]
    for step, qi in enumerate(order):
        rows = slice(qi * tq, (qi + 1) * tq)
        past = qi * tq
        sd, sp = ahead.pop(0)
        if step + DIFF_LOOKAHEAD < nq:
            ahead.append(scores(order[step + DIFF_LOOKAHEAD]))
        m = jnp.max(sd, axis=-1, keepdims=True)
        if past:
            m = jnp.maximum(m, jnp.max(sp, axis=-1, keepdims=True))
        ol = _dot(jnp.exp2(sd - m).astype(BF16), vx_ref[rows, :])
        if past:
            ol = ol + _dot(jnp.exp2(sp - m).astype(BF16), vx_ref[0:past, :])
        o0 = ol[0:tq, 0:DIFF_DV] / ol[0:tq, DIFF_DV:2 * DIFF_DV]
        o1 = ol[tq:2 * tq, 0:DIFF_DV] / ol[tq:2 * tq, DIFF_DV:2 * DIFF_DV]
        o_ref[rows, :] = (_rms(o0 - lam * o1) * gain).astype(BF16)


def _diff_attention(dq, dk, dv, lamv, subln, *, batch, seq):
    blk = pl.BlockSpec((seq, LANES), lambda b, h: (b, h))
    return pl.pallas_call(
        functools.partial(_diff_body, seq=seq),
        out_shape=jax.ShapeDtypeStruct(dv.shape, BF16),
        grid=(batch, DIFF_HEADS),
        in_specs=[blk, blk, blk, _resident(lamv.shape), _resident(subln.shape)],
        out_specs=blk,
        scratch_shapes=[pltpu.VMEM((seq, 2 * DIFF_DV), BF16)],
        compiler_params=_params(("parallel", "parallel")),
        name="diff_attention",
    )(dq, dk, dv, lamv, subln)


def _merge_body(x_ref, ro_ref, do_ref, mq_ref, mk_ref, mv_ref, nw_ref, wg0_ref, wg1_ref, wg2_ref, bg_ref,
                wro_ref, wdo_ref, wmo_ref, wout_ref, o_ref):
    x = x_ref[...]
    h = _rms(x, nw_ref[...]).astype(BF16)
    g = [jax.nn.sigmoid(_dot(h, w_ref[...]) + bg_ref[:, i * D_MODEL:(i + 1) * D_MODEL])
         for i, w_ref in enumerate((wg0_ref, wg1_ref, wg2_ref))]
    ret_out = _dot(ro_ref[...], wro_ref[...])
    diff_out = _dot(do_ref[...], wdo_ref[...])
    mem_out = None
    for hd in range(MEM_HEADS):
        cols = slice(hd * MEM_DH, (hd + 1) * MEM_DH)
        s = _dot_nt(mq_ref[:, cols], mk_ref[:, cols])
        p = jnp.exp(s - jnp.max(s, axis=-1, keepdims=True))
        p = p * (1.0 / jnp.sum(p, axis=-1, keepdims=True))
        oh = _dot(p.astype(BF16), mv_ref[:, cols])
        part = _dot(oh.astype(BF16), wmo_ref[cols, :])
        mem_out = part if mem_out is None else mem_out + part
    merged = g[0] * ret_out + g[1] * diff_out + g[2] * mem_out
    o_ref[...] = x + _dot(merged.astype(BF16), wout_ref[...])


def _merge(x1, ro, do, mq, mk, mv, nw, w_in, bg, wro, wdo, wmo, wout, *, seq, mem_len, tm=512):
    t, d = x1.shape
    per_b = seq // tm

    def row(w):
        return pl.BlockSpec((tm, w), lambda i: (i, 0))

    memb = pl.BlockSpec((mem_len, MEM_Q_W), lambda i: (i // per_b, 0))
    return pl.pallas_call(
        _merge_body,
        out_shape=jax.ShapeDtypeStruct((t, d), F32),
        grid=(t // tm,),
        in_specs=[row(d), row(RET_V_W), row(DIFF_V_W), row(MEM_Q_W), memb, memb,
                  _resident(nw.shape), _resident_cols(d, D_MODEL, QKV_W // D_MODEL),
                  _resident_cols(d, D_MODEL, QKV_W // D_MODEL + 1),
                  _resident_cols(d, D_MODEL, QKV_W // D_MODEL + 2), _resident(bg.shape), _resident(wro.shape),
                  _resident(wdo.shape), _resident(wmo.shape), _resident(wout.shape)],
        out_specs=row(d),
        compiler_params=_params(("parallel",)),
        name="merge",
    )(x1, ro, do, mq, mk, mv, nw, w_in, w_in, w_in, bg, wro, wdo, wmo, wout)


def _deinterleave_heads(w):
    d = w.shape[0]
    return w.reshape(d, RET_HEADS, RET_DK // 2, 2).transpose(0, 1, 3, 2).reshape(d, RET_QK_W)


def _rotation_tables(positions):
    pos = positions.astype(F32).reshape(-1, 1)
    ret_inv = (1.0 / (RET_THETA_BASE ** jnp.linspace(0.0, 1.0, RET_DK // 2, dtype=F32))).reshape(1, -1)
    rope_inv = 1.0 / (ROPE_THETA ** (jnp.arange(0, ROT_DIM, 2, dtype=F32) / ROT_DIM))
    d_ang = pos * rope_inv
    dtab = jnp.pad(jnp.concatenate([jnp.cos(d_ang), jnp.sin(d_ang)], axis=-1), ((0, 0), (0, LANES - ROT_DIM)))
    return pos, ret_inv, dtab


def _decay_tables():
    log_g = jnp.log(1.0 - 2.0 ** (-5.0 - jnp.arange(RET_HEADS, dtype=F32)))
    idx = jnp.arange(RET_BLOCK, dtype=F32)
    dist = jnp.abs(idx[:, None] - idx[None, :])
    chunk = jnp.arange(RET_BLOCK) // CHUNK
    visible = chunk[None, :] <= chunk[:, None]
    dmat = jnp.where(visible[None], jnp.exp(log_g[:, None, None] * dist[None]), 0.0)
    qdec = jnp.exp(log_g[:, None] * (idx[None, :] + 1.0))[..., None]
    kdec = jnp.exp(log_g[:, None] * (RET_BLOCK - 1.0 - idx[None, :]))[..., None]
    cdec = jnp.broadcast_to(jnp.exp(log_g * RET_BLOCK)[:, None, None], (RET_HEADS, 1, RET_DV))
    return dmat, qdec, kdec, cdec


def kernel(x, mem, positions, ffn1_norm, ffn1_w_gate, ffn1_w_up, ffn1_w_down, mix_norm, w_in, b_gate, ret_w_o, diff_q_norm, diff_k_norm, diff_lambda_q1, diff_lambda_k1, diff_lambda_q2, diff_lambda_k2, diff_subln, diff_w_o, mem_norm, mem_w_kv, mem_q_norm, mem_k_norm, mem_w_o, w_out, ffn2_norm, ffn2_w_gate, ffn2_w_up, ffn2_w_down, final_norm):
    batch, seq, d = x.shape
    mem_len = mem.shape[1]
    assert d == D_MODEL and seq % RET_BLOCK == 0 and seq % DIFF_QBLOCK == 0
    assert w_in.shape[0] == 1, "single-layer trunk"
    bf = lambda a: a.astype(BF16)
    vec = lambda a: a.reshape(1, -1)

    wi = bf(w_in[0])
    w_rqk = jnp.concatenate([_deinterleave_heads(wi[:, 0:RET_QK_W]),
                             _deinterleave_heads(wi[:, RET_QK_W:2 * RET_QK_W])], axis=1)
    pos, ret_inv, dtab = _rotation_tables(positions)
    dmat, qdec, kdec, cdec = _decay_tables()
    dqn = jnp.tile(vec(diff_q_norm[0]), (1, 2))
    dkn = jnp.tile(vec(diff_k_norm[0]), (1, 2))
    lamv = jnp.stack([diff_lambda_q1[0], diff_lambda_k1[0], diff_lambda_q2[0], diff_lambda_k2[0]]).astype(F32)

    xf = x.reshape(batch * seq, d)
    x1 = _ffn(xf, vec(ffn1_norm[0]), bf(ffn1_w_gate[0]), bf(ffn1_w_up[0]), bf(ffn1_w_down[0]),
              vec(final_norm[0]), final_norm=False)
    rq, rk, rv, sg, dq, dk, dv, mq = _proj(x1, vec(mix_norm[0]), w_rqk, wi, pos, ret_inv, dtab, dqn, dkn,
                                           vec(mem_q_norm[0]))
    mk, mv = _memkv(mem.reshape(batch * mem_len, d), vec(mem_norm[0]), bf(mem_w_kv[0]), vec(mem_k_norm[0]))
    ro = _retention(rq, rk, rv, sg, dmat, qdec, kdec, cdec, batch=batch, seq=seq)
    do = _diff_attention(dq, dk, dv, lamv, vec(diff_subln[0]), batch=batch, seq=seq)
    x2 = _merge(x1, ro, do, mq, mk, mv, vec(mix_norm[0]), wi, vec(b_gate[0]), bf(ret_w_o[0]),
                bf(diff_w_o[0]), bf(mem_w_o[0]), bf(w_out[0]), seq=seq, mem_len=mem_len)
    x3 = _ffn(x2, vec(ffn2_norm[0]), bf(ffn2_w_gate[0]), bf(ffn2_w_up[0]), bf(ffn2_w_down[0]),
              vec(final_norm[0]), final_norm=True)
    return x3.reshape(batch, seq, d)
```

```python
import functools
import math

import jax
import jax.numpy as jnp
from jax import lax
from jax.experimental import pallas as pl
from jax.experimental.pallas import tpu as pltpu

D_MODEL = 1024
CHUNK = 64
EPS = 1e-6
RET_HEADS = 4
RET_DK = 256
RET_DV = 512
RET_THETA_BASE = 10000.0
DIFF_HEADS = 8
DIFF_DK = 64
DIFF_DV = 128
ROPE_THETA = 500000.0
ROT_DIM = 16
MEM_HEADS = 4
MEM_DH = 256
D_FF = 2816
LAM_INIT = 0.8 - 0.6 * math.exp(-0.3 * 0)
LOG2E = math.log2(math.e)

RET_QK_W = RET_HEADS * RET_DK
RET_V_W = RET_HEADS * RET_DV
DIFF_QK_W = DIFF_HEADS * 2 * DIFF_DK
DIFF_V_W = DIFF_HEADS * DIFF_DV
MEM_Q_W = MEM_HEADS * MEM_DH
QKV_W = 2 * RET_QK_W + 2 * RET_V_W + 2 * DIFF_QK_W + DIFF_V_W + MEM_Q_W
GATE_W = 3 * D_MODEL

LANES = 128
BF16_SUBLANES = 16
RET_BLOCK = 256
DIFF_QBLOCK = 256
DIFF_HEADS_PER_STEP = 2
DIFF_LOOKAHEAD = 1
VMEM_LIMIT = 56 * 1024 * 1024

BF16 = jnp.bfloat16
F32 = jnp.float32


def _dot(a, b):
    return jnp.dot(a, b, preferred_element_type=F32)


def _dot_nt(a, b):
    return lax.dot_general(a, b, (((1,), (1,)), ((), ())), preferred_element_type=F32)


def _dot_tn(a, b):
    return lax.dot_general(a, b, (((0,), (0,)), ((), ())), preferred_element_type=F32)


def _rms(x, w=None):
    y = x * lax.rsqrt(jnp.mean(x * x, axis=-1, keepdims=True) + EPS)
    if w is not None:
        y = y * w
    return y


def _resident(shape):
    nd = len(shape)
    return pl.BlockSpec(shape, lambda *_: (0,) * nd, pipeline_mode=pl.Buffered(1))


def _resident_cols(rows, width, index):
    return pl.BlockSpec((rows, width), lambda *_: (0, index), pipeline_mode=pl.Buffered(1))


def _params(sem):
    return pltpu.CompilerParams(dimension_semantics=sem, vmem_limit_bytes=VMEM_LIMIT)


def _swiglu_half_step(x, nw_ref, wg_ref, wu_ref, wd_ref):
    h = _rms(x, nw_ref[...]).astype(BF16)
    g = _dot(h, wg_ref[...])
    u = _dot(h, wu_ref[...])
    a = (g * jax.nn.sigmoid(g) * u).astype(BF16)
    return x + 0.5 * _dot(a, wd_ref[...])


def _cast_chunks(src_refs, dst_refs):
    for src, dst in zip(src_refs, dst_refs):
        dst[...] = src[...].astype(BF16)


def _cast_specs(mats, steps):
    specs, shapes = [], []
    for m in mats:
        rows, cols = m.shape
        chunk = next(c for c in range(BF16_SUBLANES, rows + 1, BF16_SUBLANES)
                     if rows % c == 0 and rows // c <= steps)
        last = rows // chunk - 1
        specs.append(pl.BlockSpec((chunk, cols), lambda i, last=last: (jnp.minimum(i, last), 0)))
        shapes.append(jax.ShapeDtypeStruct((rows, cols), BF16))
    return specs, shapes


def _ffn_first_body(x_ref, nw_ref, wg_ref, wu_ref, wd_ref, pos_ref, rinv_ref, *rest, n_cast):
    cast_in, (o_ref, rcos_ref, rsin_ref), cast_out = rest[:n_cast], rest[n_cast:n_cast + 3], rest[n_cast + 3:]
    ang = pos_ref[...] * rinv_ref[...]
    rcos_ref[...] = jnp.cos(ang)
    rsin_ref[...] = jnp.sin(ang)
    _cast_chunks(cast_in, cast_out)
    o_ref[...] = _swiglu_half_step(x_ref[...], nw_ref, wg_ref, wu_ref, wd_ref)


def _ffn_first(x, nw, wg, wu, wd, pos, rinv, cast, *, tm=512):
    t, d = x.shape
    steps = t // tm
    row = pl.BlockSpec((tm, d), lambda i: (i, 0))
    tab = pl.BlockSpec((tm, LANES), lambda i: (i, 0))
    cast_specs, cast_shapes = _cast_specs(cast, steps)
    outs = pl.pallas_call(
        functools.partial(_ffn_first_body, n_cast=len(cast)),
        out_shape=(jax.ShapeDtypeStruct((t, d), F32), jax.ShapeDtypeStruct((t, LANES), F32),
                   jax.ShapeDtypeStruct((t, LANES), F32), *cast_shapes),
        grid=(steps,),
        in_specs=[row, _resident(nw.shape), _resident(wg.shape), _resident(wu.shape), _resident(wd.shape),
                  pl.BlockSpec((tm, 1), lambda i: (i, 0)), _resident(rinv.shape), *cast_specs],
        out_specs=(row, tab, tab, *cast_specs),
        compiler_params=_params(("arbitrary",)),
        name="ffn_first",
    )(x, nw, wg, wu, wd, pos, rinv, *cast)
    return outs[0], outs[1], outs[2], outs[3:]


def _ffn_last_body(x_ref, nw_ref, wg_ref, wu_ref, wd_ref, fw_ref, o_ref):
    o_ref[...] = _rms(_swiglu_half_step(x_ref[...], nw_ref, wg_ref, wu_ref, wd_ref), fw_ref[...])


def _ffn_last(x, nw, wg, wu, wd, fw, *, tm=512):
    t, d = x.shape
    row = pl.BlockSpec((tm, d), lambda i: (i, 0))
    return pl.pallas_call(
        _ffn_last_body,
        out_shape=jax.ShapeDtypeStruct((t, d), F32),
        grid=(t // tm,),
        in_specs=[row, _resident(nw.shape), _resident(wg.shape), _resident(wu.shape),
                  _resident(wd.shape), _resident(fw.shape)],
        out_specs=row,
        compiler_params=_params(("parallel",)),
        name="ffn_last",
    )(x, nw, wg, wu, wd, fw)


def _proj_body(x_ref, nw_ref, wrqk_ref, wrv_ref, wrg_ref, wdq_ref, wdk_ref, wdv_ref, wmq_ref,
               rcos_ref, rsin_ref, dtab_ref, dqn_ref, dkn_ref, mqn_ref, *rest, n_cast):
    cast_in, cast_out = rest[:n_cast], rest[n_cast + 8:]
    rq_ref, rk_ref, rv_ref, sg_ref, dq_ref, dk_ref, dv_ref, mq_ref = rest[n_cast:n_cast + 8]
    h = _rms(x_ref[...], nw_ref[...]).astype(BF16)
    lane = lax.broadcasted_iota(jnp.int32, (1, LANES), 1)
    half = ROT_DIM // 2
    rcos = rcos_ref[...]
    rsin = rsin_ref[...]

    dtab = dtab_ref[...]
    c = jnp.where(lane < half, dtab, 0.0)
    c = c + pltpu.roll(c, half, 1)
    c = c + pltpu.roll(c, DIFF_DK, 1)
    dcos = jnp.where((lane % DIFF_DK) < ROT_DIM, c, 1.0)
    sn = jnp.where((lane >= half) & (lane < ROT_DIM), dtab, 0.0)
    sn = sn - pltpu.roll(sn, LANES - half, 1)
    dsin = sn + pltpu.roll(sn, DIFF_DK, 1)

    def ret_rotate(out_ref, scale):
        def epilogue(y):
            for hd in range(RET_HEADS):
                ye = y[:, hd * RET_DK:hd * RET_DK + LANES]
                yo = y[:, hd * RET_DK + LANES:(hd + 1) * RET_DK]
                out_ref[:, hd * RET_DK:hd * RET_DK + LANES] = ((ye * rcos - yo * rsin) * scale).astype(BF16)
                out_ref[:, hd * RET_DK + LANES:(hd + 1) * RET_DK] = ((yo * rcos + ye * rsin) * scale).astype(BF16)
        return epilogue

    def store(out_ref):
        def epilogue(y):
            out_ref[...] = y.astype(BF16)
        return epilogue

    def silu_store(y):
        sg_ref[...] = (y * jax.nn.sigmoid(y)).astype(BF16)

    low = lane < DIFF_DK
    first = (lane % DIFF_DK) < half

    def diff_norm_rope(out_ref, n_ref, scale):
        def epilogue(y):
            gain = n_ref[...]
            for hd in range(DIFF_HEADS):
                yh = y[:, hd * LANES:(hd + 1) * LANES]
                sq = yh * yh
                ss_lo = jnp.sum(jnp.where(low, sq, 0.0), axis=-1, keepdims=True)
                ss_hi = jnp.sum(jnp.where(low, 0.0, sq), axis=-1, keepdims=True)
                r = jnp.where(low, lax.rsqrt(ss_lo * (1.0 / DIFF_DK) + EPS),
                              lax.rsqrt(ss_hi * (1.0 / DIFF_DK) + EPS))
                yn = yh * r * gain
                partner = jnp.where(first, pltpu.roll(yn, LANES - half, 1), pltpu.roll(yn, half, 1))
                out_ref[:, hd * LANES:(hd + 1) * LANES] = ((yn * dcos + partner * dsin) * scale).astype(BF16)
        return epilogue

    def mem_norm(y):
        gain = mqn_ref[...]
        for hd in range(MEM_HEADS):
            yh = y[:, hd * MEM_DH:(hd + 1) * MEM_DH]
            mq_ref[:, hd * MEM_DH:(hd + 1) * MEM_DH] = (_rms(yh) * gain).astype(BF16)

    stages = (
        (lambda: wrqk_ref[:, 0:RET_QK_W], ret_rotate(rq_ref, 1.0)),
        (lambda: wrv_ref[...], store(rv_ref)),
        (lambda: wrqk_ref[:, RET_QK_W:2 * RET_QK_W], ret_rotate(rk_ref, RET_DK ** -0.5)),
        (lambda: wrg_ref[...], silu_store),
        (lambda: wdq_ref[...], diff_norm_rope(dq_ref, dqn_ref, LOG2E * DIFF_DK ** -0.5)),
        (lambda: wmq_ref[...], mem_norm),
        (lambda: wdk_ref[...], diff_norm_rope(dk_ref, dkn_ref, 1.0)),
        (lambda: wdv_ref[...], store(dv_ref)),
    )
    y_next = _dot(h, stages[0][0]())
    for i, (_, epilogue) in enumerate(stages):
        y = y_next
        if i + 1 < len(stages):
            y_next = _dot(h, stages[i + 1][0]())
        epilogue(y)
    _cast_chunks(cast_in, cast_out)


def _proj(x1, nw, w_rqk, w_in, rcos, rsin, dtab, dqn, dkn, mqn, cast, *, tm=256):
    t, d = x1.shape
    cast_specs, cast_shapes = _cast_specs(cast, t // tm)
    widths = (RET_QK_W, RET_QK_W, RET_V_W, RET_V_W, DIFF_QK_W, DIFF_QK_W, DIFF_V_W, MEM_Q_W)

    def row(w):
        return pl.BlockSpec((tm, w), lambda i: (i, 0))

    outs = pl.pallas_call(
        functools.partial(_proj_body, n_cast=len(cast)),
        out_shape=(*(jax.ShapeDtypeStruct((t, w), BF16) for w in widths), *cast_shapes),
        grid=(t // tm,),
        in_specs=[row(d), _resident(nw.shape), _resident(w_rqk.shape),
                  _resident_cols(d, RET_V_W, 1), _resident_cols(d, RET_V_W, 2),
                  _resident_cols(d, DIFF_QK_W, 6), _resident_cols(d, DIFF_QK_W, 7),
                  _resident_cols(d, DIFF_V_W, 8), _resident_cols(d, MEM_Q_W, 9),
                  row(LANES), row(LANES), row(LANES),
                  _resident(dqn.shape), _resident(dkn.shape), _resident(mqn.shape), *cast_specs],
        out_specs=(*(row(w) for w in widths), *cast_specs),
        compiler_params=_params(("arbitrary",)),
        name="proj",
    )(x1, nw, w_rqk, w_in, w_in, w_in, w_in, w_in, w_in, rcos, rsin, dtab, dqn, dkn, mqn, *cast)
    return outs[:len(widths)], outs[len(widths):]


def _memkv_body(m_ref, nw_ref, w_ref, kn_ref, mk_ref, mv_ref):
    hm = _rms(m_ref[...], nw_ref[...]).astype(BF16)
    k = _dot(hm, w_ref[:, 0:MEM_Q_W])
    gain = kn_ref[...] * (MEM_DH ** -0.5)
    for hd in range(MEM_HEADS):
        kh = k[:, hd * MEM_DH:(hd + 1) * MEM_DH]
        mk_ref[:, hd * MEM_DH:(hd + 1) * MEM_DH] = (_rms(kh) * gain).astype(BF16)
    mv_ref[...] = _dot(hm, w_ref[:, MEM_Q_W:2 * MEM_Q_W]).astype(BF16)


def _memkv(mem2d, nw, w_kv, kn, *, tm=256):
    t, d = mem2d.shape
    row = pl.BlockSpec((tm, d), lambda i: (i, 0))
    out = pl.BlockSpec((tm, MEM_Q_W), lambda i: (i, 0))
    return pl.pallas_call(
        _memkv_body,
        out_shape=(jax.ShapeDtypeStruct((t, MEM_Q_W), BF16),) * 2,
        grid=(t // tm,),
        in_specs=[row, _resident(nw.shape), _resident(w_kv.shape), _resident(kn.shape)],
        out_specs=(out, out),
        compiler_params=_params(("parallel",)),
        name="memkv",
    )(mem2d, nw, w_kv, kn)


def _ret_body(q_ref, k_ref, v_ref, sg_ref, dmat_ref, qdec_ref, kdec_ref, cdec_ref, o_ref, state_ref, *, seq):
    dmat = dmat_ref[0]
    qdec = qdec_ref[0]
    kdec = kdec_ref[0]
    cdec = cdec_ref[0]

    def local(n):
        rows = slice(n * RET_BLOCK, (n + 1) * RET_BLOCK)
        q = q_ref[rows, :]
        k = k_ref[rows, :]
        v = v_ref[rows, :]
        s = _dot_nt(q, k) * dmat
        o = _dot(s.astype(BF16), v)
        upd = _dot_tn((k.astype(F32) * kdec).astype(BF16), v)
        qd = (q.astype(F32) * qdec).astype(BF16) if n else None
        return o, upd, qd

    nblk = seq // RET_BLOCK
    nxt = local(0)
    for n in range(nblk):
        rows = slice(n * RET_BLOCK, (n + 1) * RET_BLOCK)
        o, upd, qd = nxt
        if n + 1 < nblk:
            nxt = local(n + 1)
        if n == 0:
            state_ref[...] = upd
        else:
            st = state_ref[...]
            o = o + _dot(qd, st.astype(BF16))
            state_ref[...] = st * cdec + upd
        o_ref[rows, :] = (_rms(o) * sg_ref[rows, :].astype(F32)).astype(BF16)


def _retention(rq, rk, rv, sg, dmat, qdec, kdec, cdec, *, batch, seq):
    qk = pl.BlockSpec((seq, RET_DK), lambda b, h: (b, h))
    vv = pl.BlockSpec((seq, RET_DV), lambda b, h: (b, h))

    def per_head(shape):
        return pl.BlockSpec((1,) + shape[1:], lambda b, h: (h, 0, 0))

    return pl.pallas_call(
        functools.partial(_ret_body, seq=seq),
        out_shape=jax.ShapeDtypeStruct(rv.shape, BF16),
        grid=(batch, RET_HEADS),
        in_specs=[qk, qk, vv, vv, per_head(dmat.shape), per_head(qdec.shape), per_head(kdec.shape),
                  per_head(cdec.shape)],
        out_specs=vv,
        scratch_shapes=[pltpu.VMEM((RET_DK, RET_DV), F32)],
        compiler_params=_params(("parallel", "parallel")),
        name="retention",
    )(rq, rk, rv, sg, dmat, qdec, kdec, cdec)


def _diff_body(q_ref, k_ref, v_ref, lamv_ref, subln_ref, o_ref, vx_ref, *, seq):
    lv = lamv_ref[...]
    lam = (jnp.exp(jnp.sum(lv[0:1] * lv[1:2], axis=-1, keepdims=True))
           - jnp.exp(jnp.sum(lv[2:3] * lv[3:4], axis=-1, keepdims=True)) + LAM_INIT)
    gain = subln_ref[...] * (1.0 - LAM_INIT)
    tq = DIFF_QBLOCK
    low = lax.broadcasted_iota(jnp.int32, (1, LANES), 1) < DIFF_DK
    rchunk = (lax.broadcasted_iota(jnp.int32, (2 * tq, tq), 0) % tq) // CHUNK
    cchunk = lax.broadcasted_iota(jnp.int32, (2 * tq, tq), 1) // CHUNK
    visible = cchunk <= rchunk
    zero = jnp.zeros((), BF16)

    nq = seq // tq
    for hd in range(DIFF_HEADS_PER_STEP):
        vx_ref[hd, :, 0:DIFF_DV] = v_ref[:, hd * DIFF_DV:(hd + 1) * DIFF_DV]
        vx_ref[hd, :, DIFF_DV:2 * DIFF_DV] = jnp.ones((seq, DIFF_DV), BF16)

    def scores(item):
        hd, qi = item
        cols = slice(hd * LANES, (hd + 1) * LANES)
        q = q_ref[qi * tq:(qi + 1) * tq, cols]
        q2 = jnp.concatenate([jnp.where(low, q, zero), jnp.where(low, zero, q)], axis=0)
        sd = jnp.where(visible, _dot_nt(q2, k_ref[qi * tq:(qi + 1) * tq, cols]), -jnp.inf)
        sp = _dot_nt(q2, k_ref[0:qi * tq, cols]) if qi else None
        return sd, sp

    items = [(hd, qi) for hd in range(DIFF_HEADS_PER_STEP) for qi in reversed(range(nq))]
    ahead = [scores(item) for item in items[:DIFF_LOOKAHEAD]]
    for step, (hd, qi) in enumerate(items):
        rows = slice(qi * tq, (qi + 1) * tq)
        past = qi * tq
        sd, sp = ahead.pop(0)
        if step + DIFF_LOOKAHEAD < len(items):
            ahead.append(scores(items[step + DIFF_LOOKAHEAD]))
        m = jnp.max(sd, axis=-1, keepdims=True)
        if past:
            m = jnp.maximum(m, jnp.max(sp, axis=-1, keepdims=True))
        ol = _dot(jnp.exp2(sd - m).astype(BF16), vx_ref[hd, rows, :])
        if past:
            ol = ol + _dot(jnp.exp2(sp - m).astype(BF16), vx_ref[hd, 0:past, :])
        o0 = ol[0:tq, 0:DIFF_DV] / ol[0:tq, DIFF_DV:2 * DIFF_DV]
        o1 = ol[tq:2 * tq, 0:DIFF_DV] / ol[tq:2 * tq, DIFF_DV:2 * DIFF_DV]
        o_ref[rows, hd * DIFF_DV:(hd + 1) * DIFF_DV] = (_rms(o0 - lam * o1) * gain).astype(BF16)


def _diff_attention(dq, dk, dv, lamv, subln, *, batch, seq):
    blk = pl.BlockSpec((seq, DIFF_HEADS_PER_STEP * LANES), lambda b, h: (b, h))
    return pl.pallas_call(
        functools.partial(_diff_body, seq=seq),
        out_shape=jax.ShapeDtypeStruct(dv.shape, BF16),
        grid=(batch, DIFF_HEADS // DIFF_HEADS_PER_STEP),
        in_specs=[blk, blk, blk, _resident(lamv.shape), _resident(subln.shape)],
        out_specs=blk,
        scratch_shapes=[pltpu.VMEM((DIFF_HEADS_PER_STEP, seq, 2 * DIFF_DV), BF16)],
        compiler_params=_params(("parallel", "parallel")),
        name="diff_attention",
    )(dq, dk, dv, lamv, subln)


def _merge_body(x_ref, ro_ref, do_ref, mq_ref, mk_ref, mv_ref, nw_ref, wg0_ref, wg1_ref, wg2_ref, bg_ref,
                wro_ref, wdo_ref, wmo_ref, wout_ref, o_ref):
    x = x_ref[...]
    h = _rms(x, nw_ref[...]).astype(BF16)
    g = [jax.nn.sigmoid(_dot(h, w_ref[...]) + bg_ref[:, i * D_MODEL:(i + 1) * D_MODEL])
         for i, w_ref in enumerate((wg0_ref, wg1_ref, wg2_ref))]
    ret_out = _dot(ro_ref[...], wro_ref[...])
    diff_out = _dot(do_ref[...], wdo_ref[...])
    mem_out = None
    for hd in range(MEM_HEADS):
        cols = slice(hd * MEM_DH, (hd + 1) * MEM_DH)
        s = _dot_nt(mq_ref[:, cols], mk_ref[:, cols])
        p = jnp.exp(s - jnp.max(s, axis=-1, keepdims=True))
        p = p * (1.0 / jnp.sum(p, axis=-1, keepdims=True))
        oh = _dot(p.astype(BF16), mv_ref[:, cols])
        part = _dot(oh.astype(BF16), wmo_ref[cols, :])
        mem_out = part if mem_out is None else mem_out + part
    merged = g[0] * ret_out + g[1] * diff_out + g[2] * mem_out
    o_ref[...] = x + _dot(merged.astype(BF16), wout_ref[...])


def _merge(x1, ro, do, mq, mk, mv, nw, w_in, bg, wro, wdo, wmo, wout, *, seq, mem_len, tm=512):
    t, d = x1.shape
    per_b = seq // tm

    def row(w):
        return pl.BlockSpec((tm, w), lambda i: (i, 0))

    memb = pl.BlockSpec((mem_len, MEM_Q_W), lambda i: (i // per_b, 0))
    return pl.pallas_call(
        _merge_body,
        out_shape=jax.ShapeDtypeStruct((t, d), F32),
        grid=(t // tm,),
        in_specs=[row(d), row(RET_V_W), row(DIFF_V_W), row(MEM_Q_W), memb, memb,
                  _resident(nw.shape), _resident_cols(d, D_MODEL, QKV_W // D_MODEL),
                  _resident_cols(d, D_MODEL, QKV_W // D_MODEL + 1),
                  _resident_cols(d, D_MODEL, QKV_W // D_MODEL + 2), _resident(bg.shape), _resident(wro.shape),
                  _resident(wdo.shape), _resident(wmo.shape), _resident(wout.shape)],
        out_specs=row(d),
        compiler_params=_params(("parallel",)),
        name="merge",
    )(x1, ro, do, mq, mk, mv, nw, w_in, w_in, w_in, bg, wro, wdo, wmo, wout)


def _deinterleave_heads(w):
    d = w.shape[0]
    return w.reshape(d, RET_HEADS, RET_DK // 2, 2).transpose(0, 1, 3, 2).reshape(d, RET_QK_W)


def _rotation_tables(positions):
    pos = positions.astype(F32).reshape(-1, 1)
    ret_inv = (1.0 / (RET_THETA_BASE ** jnp.linspace(0.0, 1.0, RET_DK // 2, dtype=F32))).reshape(1, -1)
    rope_inv = 1.0 / (ROPE_THETA ** (jnp.arange(0, ROT_DIM, 2, dtype=F32) / ROT_DIM))
    d_ang = pos * rope_inv
    dtab = jnp.pad(jnp.concatenate([jnp.cos(d_ang), jnp.sin(d_ang)], axis=-1), ((0, 0), (0, LANES - ROT_DIM)))
    return pos, ret_inv, dtab


def _decay_tables():
    log_g = jnp.log(1.0 - 2.0 ** (-5.0 - jnp.arange(RET_HEADS, dtype=F32)))
    idx = jnp.arange(RET_BLOCK, dtype=F32)
    dist = jnp.abs(idx[:, None] - idx[None, :])
    chunk = jnp.arange(RET_BLOCK) // CHUNK
    visible = chunk[None, :] <= chunk[:, None]
    dmat = jnp.where(visible[None], jnp.exp(log_g[:, None, None] * dist[None]), 0.0)
    qdec = jnp.exp(log_g[:, None] * (idx[None, :] + 1.0))[..., None]
    kdec = jnp.exp(log_g[:, None] * (RET_BLOCK - 1.0 - idx[None, :]))[..., None]
    cdec = jnp.broadcast_to(jnp.exp(log_g * RET_BLOCK)[:, None, None], (RET_HEADS, 1, RET_DV))
    return dmat, qdec, kdec, cdec


def kernel(x, mem, positions, ffn1_norm, ffn1_w_gate, ffn1_w_up, ffn1_w_down, mix_norm, w_in, b_gate, ret_w_o, diff_q_norm, diff_k_norm, diff_lambda_q1, diff_lambda_k1, diff_lambda_q2, diff_lambda_k2, diff_subln, diff_w_o, mem_norm, mem_w_kv, mem_q_norm, mem_k_norm, mem_w_o, w_out, ffn2_norm, ffn2_w_gate, ffn2_w_up, ffn2_w_down, final_norm):
    batch, seq, d = x.shape
    mem_len = mem.shape[1]
    assert d == D_MODEL and seq % RET_BLOCK == 0 and seq % DIFF_QBLOCK == 0
    assert w_in.shape[0] == 1, "single-layer trunk"
    bf = lambda a: a.astype(BF16)
    vec = lambda a: a.reshape(1, -1)

    w_rqk = bf(jnp.concatenate([_deinterleave_heads(w_in[0][:, 0:RET_QK_W]),
                                _deinterleave_heads(w_in[0][:, RET_QK_W:2 * RET_QK_W])], axis=1))
    pos, ret_inv, dtab = _rotation_tables(positions)
    dmat, qdec, kdec, cdec = _decay_tables()
    dqn = jnp.tile(vec(diff_q_norm[0]), (1, 2))
    dkn = jnp.tile(vec(diff_k_norm[0]), (1, 2))
    lamv = jnp.stack([diff_lambda_q1[0], diff_lambda_k1[0], diff_lambda_q2[0], diff_lambda_k2[0]]).astype(F32)

    xf = x.reshape(batch * seq, d)
    x1, rcos, rsin, (wi, w_kv, w_ro, w_do, w_mo, w_o) = _ffn_first(
        xf, vec(ffn1_norm[0]), bf(ffn1_w_gate[0]), bf(ffn1_w_up[0]), bf(ffn1_w_down[0]), pos, ret_inv,
        (w_in[0], mem_w_kv[0], ret_w_o[0], diff_w_o[0], mem_w_o[0], w_out[0]))
    (rq, rk, rv, sg, dq, dk, dv, mq), (w2g, w2u, w2d) = _proj(
        x1, vec(mix_norm[0]), w_rqk, wi, rcos, rsin, dtab, dqn, dkn, vec(mem_q_norm[0]),
        (ffn2_w_gate[0], ffn2_w_up[0], ffn2_w_down[0]))
    mk, mv = _memkv(mem.reshape(batch * mem_len, d), vec(mem_norm[0]), w_kv, vec(mem_k_norm[0]))
    ro = _retention(rq, rk, rv, sg, dmat, qdec, kdec, cdec, batch=batch, seq=seq)
    do = _diff_attention(dq, dk, dv, lamv, vec(diff_subln[0]), batch=batch, seq=seq)
    x2 = _merge(x1, ro, do, mq, mk, mv, vec(mix_norm[0]), wi, vec(b_gate[0]), w_ro, w_do, w_mo, w_o,
                seq=seq, mem_len=mem_len)
    x3 = _ffn_last(x2, vec(ffn2_norm[0]), w2g, w2u, w2d, vec(final_norm[0]))
    return x3.reshape(batch, seq, d)
```

```python
import functools
import math

import jax
import jax.numpy as jnp
from jax import lax
from jax.experimental import pallas as pl
from jax.experimental.pallas import tpu as pltpu

D_MODEL = 1024
CHUNK = 64
EPS = 1e-6
RET_HEADS = 4
RET_DK = 256
RET_DV = 512
RET_THETA_BASE = 10000.0
DIFF_HEADS = 8
DIFF_DK = 64
DIFF_DV = 128
ROPE_THETA = 500000.0
ROT_DIM = 16
MEM_HEADS = 4
MEM_DH = 256
D_FF = 2816
LAM_INIT = 0.8 - 0.6 * math.exp(-0.3 * 0)
LOG2E = math.log2(math.e)

RET_QK_W = RET_HEADS * RET_DK
RET_V_W = RET_HEADS * RET_DV
DIFF_QK_W = DIFF_HEADS * 2 * DIFF_DK
DIFF_V_W = DIFF_HEADS * DIFF_DV
MEM_Q_W = MEM_HEADS * MEM_DH
QKV_W = 2 * RET_QK_W + 2 * RET_V_W + 2 * DIFF_QK_W + DIFF_V_W + MEM_Q_W
GATE_W = 3 * D_MODEL

LANES = 128
BF16_SUBLANES = 16
RET_BLOCK = 256
DIFF_QBLOCK = 256
DIFF_HEADS_PER_STEP = 2
DIFF_LOOKAHEAD = 1
VMEM_LIMIT = 56 * 1024 * 1024

BF16 = jnp.bfloat16
F32 = jnp.float32


def _dot(a, b):
    return jnp.dot(a, b, preferred_element_type=F32)


def _dot_nt(a, b):
    return lax.dot_general(a, b, (((1,), (1,)), ((), ())), preferred_element_type=F32)


def _dot_tn(a, b):
    return lax.dot_general(a, b, (((0,), (0,)), ((), ())), preferred_element_type=F32)


def _rms(x, w=None):
    y = x * lax.rsqrt(jnp.mean(x * x, axis=-1, keepdims=True) + EPS)
    if w is not None:
        y = y * w
    return y


def _resident(shape):
    nd = len(shape)
    return pl.BlockSpec(shape, lambda *_: (0,) * nd, pipeline_mode=pl.Buffered(1))


def _resident_cols(rows, width, index):
    return pl.BlockSpec((rows, width), lambda *_: (0, index), pipeline_mode=pl.Buffered(1))


def _params(sem):
    return pltpu.CompilerParams(dimension_semantics=sem, vmem_limit_bytes=VMEM_LIMIT)


def _swiglu_half_step(x, nw_ref, wg_ref, wu_ref, wd_ref):
    h = _rms(x, nw_ref[...]).astype(BF16)
    g = _dot(h, wg_ref[...])
    u = _dot(h, wu_ref[...])
    a = (g * jax.nn.sigmoid(g) * u).astype(BF16)
    return x + 0.5 * _dot(a, wd_ref[...])


def _cast_chunks(src_refs, dst_refs):
    for src, dst in zip(src_refs, dst_refs):
        dst[...] = src[...].astype(BF16)


def _cast_specs(mats, steps):
    specs, shapes = [], []
    for m in mats:
        rows, cols = m.shape
        chunk = next(c for c in range(BF16_SUBLANES, rows + 1, BF16_SUBLANES)
                     if rows % c == 0 and rows // c <= steps)
        last = rows // chunk - 1
        specs.append(pl.BlockSpec((chunk, cols), lambda i, last=last: (jnp.minimum(i, last), 0)))
        shapes.append(jax.ShapeDtypeStruct((rows, cols), BF16))
    return specs, shapes


def _ffn_first_body(x_ref, nw_ref, wg_ref, wu_ref, wd_ref, pos_ref, rinv_ref, perm_ref, *rest, n_cast):
    cast_in, (o_ref, rcos_ref, rsin_ref, wrqk_ref), cast_out = (
        rest[:n_cast], rest[n_cast:n_cast + 4], rest[n_cast + 4:])
    ang = pos_ref[...] * rinv_ref[...]
    rcos_ref[...] = jnp.cos(ang)
    rsin_ref[...] = jnp.sin(ang)
    _cast_chunks(cast_in, cast_out)
    perm = perm_ref[...]
    for hb in range(2 * RET_HEADS):
        cols = slice(hb * RET_DK, (hb + 1) * RET_DK)
        wrqk_ref[:, cols] = _dot(cast_in[0][:, cols].astype(BF16), perm).astype(BF16)
    o_ref[...] = _swiglu_half_step(x_ref[...], nw_ref, wg_ref, wu_ref, wd_ref)


def _ffn_first(x, nw, wg, wu, wd, pos, rinv, perm, cast, *, tm=512):
    t, d = x.shape
    steps = t // tm
    row = pl.BlockSpec((tm, d), lambda i: (i, 0))
    tab = pl.BlockSpec((tm, LANES), lambda i: (i, 0))
    cast_specs, cast_shapes = _cast_specs(cast, steps)
    w_in_chunk = cast_specs[0].block_shape[0]
    assert cast[0].shape[0] == w_in_chunk * steps, "every step must see a fresh row chunk of w_in"
    wrqk_spec = pl.BlockSpec((w_in_chunk, 2 * RET_QK_W), lambda i: (i, 0))
    outs = pl.pallas_call(
        functools.partial(_ffn_first_body, n_cast=len(cast)),
        out_shape=(jax.ShapeDtypeStruct((t, d), F32), jax.ShapeDtypeStruct((t, LANES), F32),
                   jax.ShapeDtypeStruct((t, LANES), F32),
                   jax.ShapeDtypeStruct((cast[0].shape[0], 2 * RET_QK_W), BF16), *cast_shapes),
        grid=(steps,),
        in_specs=[row, _resident(nw.shape), _resident(wg.shape), _resident(wu.shape), _resident(wd.shape),
                  pl.BlockSpec((tm, 1), lambda i: (i, 0)), _resident(rinv.shape), _resident(perm.shape),
                  *cast_specs],
        out_specs=(row, tab, tab, wrqk_spec, *cast_specs),
        compiler_params=_params(("arbitrary",)),
        name="ffn_first",
    )(x, nw, wg, wu, wd, pos, rinv, perm, *cast)
    return outs[0], outs[1], outs[2], outs[3], outs[4:]


def _ffn_last_body(x_ref, nw_ref, wg_ref, wu_ref, wd_ref, fw_ref, o_ref):
    o_ref[...] = _rms(_swiglu_half_step(x_ref[...], nw_ref, wg_ref, wu_ref, wd_ref), fw_ref[...])


def _ffn_last(x, nw, wg, wu, wd, fw, *, tm=512):
    t, d = x.shape
    row = pl.BlockSpec((tm, d), lambda i: (i, 0))
    return pl.pallas_call(
        _ffn_last_body,
        out_shape=jax.ShapeDtypeStruct((t, d), F32),
        grid=(t // tm,),
        in_specs=[row, _resident(nw.shape), _resident(wg.shape), _resident(wu.shape),
                  _resident(wd.shape), _resident(fw.shape)],
        out_specs=row,
        compiler_params=_params(("parallel",)),
        name="ffn_last",
    )(x, nw, wg, wu, wd, fw)


def _proj_body(x_ref, nw_ref, wrqk_ref, wrv_ref, wdq_ref, wdk_ref, wdv_ref, wmq_ref,
               rcos_ref, rsin_ref, dtab_ref, dqn_ref, dkn_ref, mqn_ref, *rest, n_cast):
    cast_in, cast_out = rest[:n_cast], rest[n_cast + 7:]
    rq_ref, rk_ref, rv_ref, dq_ref, dk_ref, dv_ref, mq_ref = rest[n_cast:n_cast + 7]
    h = _rms(x_ref[...], nw_ref[...]).astype(BF16)
    lane = lax.broadcasted_iota(jnp.int32, (1, LANES), 1)
    half = ROT_DIM // 2
    rcos = rcos_ref[...]
    rsin = rsin_ref[...]

    dtab = dtab_ref[...]
    c = jnp.where(lane < half, dtab, 0.0)
    c = c + pltpu.roll(c, half, 1)
    c = c + pltpu.roll(c, DIFF_DK, 1)
    dcos = jnp.where((lane % DIFF_DK) < ROT_DIM, c, 1.0)
    sn = jnp.where((lane >= half) & (lane < ROT_DIM), dtab, 0.0)
    sn = sn - pltpu.roll(sn, LANES - half, 1)
    dsin = sn + pltpu.roll(sn, DIFF_DK, 1)

    def ret_rotate(out_ref, scale):
        def epilogue(y):
            for hd in range(RET_HEADS):
                ye = y[:, hd * RET_DK:hd * RET_DK + LANES]
                yo = y[:, hd * RET_DK + LANES:(hd + 1) * RET_DK]
                out_ref[:, hd * RET_DK:hd * RET_DK + LANES] = ((ye * rcos - yo * rsin) * scale).astype(BF16)
                out_ref[:, hd * RET_DK + LANES:(hd + 1) * RET_DK] = ((yo * rcos + ye * rsin) * scale).astype(BF16)
        return epilogue

    def store(out_ref):
        def epilogue(y):
            out_ref[...] = y.astype(BF16)
        return epilogue

    low = lane < DIFF_DK
    first = (lane % DIFF_DK) < half

    def diff_norm_rope(out_ref, n_ref, scale):
        def epilogue(y):
            gain = n_ref[...]
            for hd in range(DIFF_HEADS):
                yh = y[:, hd * LANES:(hd + 1) * LANES]
                sq = yh * yh
                ss_lo = jnp.sum(jnp.where(low, sq, 0.0), axis=-1, keepdims=True)
                ss_hi = jnp.sum(jnp.where(low, 0.0, sq), axis=-1, keepdims=True)
                r = jnp.where(low, lax.rsqrt(ss_lo * (1.0 / DIFF_DK) + EPS),
                              lax.rsqrt(ss_hi * (1.0 / DIFF_DK) + EPS))
                yn = yh * r * gain
                partner = jnp.where(first, pltpu.roll(yn, LANES - half, 1), pltpu.roll(yn, half, 1))
                out_ref[:, hd * LANES:(hd + 1) * LANES] = ((yn * dcos + partner * dsin) * scale).astype(BF16)
        return epilogue

    def mem_norm(y):
        gain = mqn_ref[...]
        for hd in range(MEM_HEADS):
            yh = y[:, hd * MEM_DH:(hd + 1) * MEM_DH]
            mq_ref[:, hd * MEM_DH:(hd + 1) * MEM_DH] = (_rms(yh) * gain).astype(BF16)

    stages = (
        (lambda: wrqk_ref[:, 0:RET_QK_W], ret_rotate(rq_ref, 1.0)),
        (lambda: wrv_ref[...], store(rv_ref)),
        (lambda: wrqk_ref[:, RET_QK_W:2 * RET_QK_W], ret_rotate(rk_ref, RET_DK ** -0.5)),
        (lambda: wdq_ref[...], diff_norm_rope(dq_ref, dqn_ref, LOG2E * DIFF_DK ** -0.5)),
        (lambda: wmq_ref[...], mem_norm),
        (lambda: wdk_ref[...], diff_norm_rope(dk_ref, dkn_ref, 1.0)),
        (lambda: wdv_ref[...], store(dv_ref)),
    )
    y_next = _dot(h, stages[0][0]())
    for i, (_, epilogue) in enumerate(stages):
        y = y_next
        if i + 1 < len(stages):
            y_next = _dot(h, stages[i + 1][0]())
        epilogue(y)
    _cast_chunks(cast_in, cast_out)


def _proj(x1, nw, w_rqk, w_in, rcos, rsin, dtab, dqn, dkn, mqn, cast, *, tm=256):
    t, d = x1.shape
    cast_specs, cast_shapes = _cast_specs(cast, t // tm)
    widths = (RET_QK_W, RET_QK_W, RET_V_W, DIFF_QK_W, DIFF_QK_W, DIFF_V_W, MEM_Q_W)

    def row(w):
        return pl.BlockSpec((tm, w), lambda i: (i, 0))

    outs = pl.pallas_call(
        functools.partial(_proj_body, n_cast=len(cast)),
        out_shape=(*(jax.ShapeDtypeStruct((t, w), BF16) for w in widths), *cast_shapes),
        grid=(t // tm,),
        in_specs=[row(d), _resident(nw.shape), _resident(w_rqk.shape),
                  _resident_cols(d, RET_V_W, 1),
                  _resident_cols(d, DIFF_QK_W, 6), _resident_cols(d, DIFF_QK_W, 7),
                  _resident_cols(d, DIFF_V_W, 8), _resident_cols(d, MEM_Q_W, 9),
                  row(LANES), row(LANES), row(LANES),
                  _resident(dqn.shape), _resident(dkn.shape), _resident(mqn.shape), *cast_specs],
        out_specs=(*(row(w) for w in widths), *cast_specs),
        compiler_params=_params(("arbitrary",)),
        name="proj",
    )(x1, nw, w_rqk, w_in, w_in, w_in, w_in, w_in, rcos, rsin, dtab, dqn, dkn, mqn, *cast)
    return outs[:len(widths)], outs[len(widths):]


def _memkv_body(m_ref, nw_ref, w_ref, kn_ref, *rest, n_cast):
    cast_in, (mk_ref, mv_ref), cast_out = rest[:n_cast], rest[n_cast:n_cast + 2], rest[n_cast + 2:]
    _cast_chunks(cast_in, cast_out)
    hm = _rms(m_ref[...], nw_ref[...]).astype(BF16)
    k = _dot(hm, w_ref[:, 0:MEM_Q_W].astype(BF16))
    gain = kn_ref[...] * (MEM_DH ** -0.5)
    for hd in range(MEM_HEADS):
        kh = k[:, hd * MEM_DH:(hd + 1) * MEM_DH]
        mk_ref[:, hd * MEM_DH:(hd + 1) * MEM_DH] = (_rms(kh) * gain).astype(BF16)
    mv_ref[...] = _dot(hm, w_ref[:, MEM_Q_W:2 * MEM_Q_W].astype(BF16)).astype(BF16)


def _memkv(mem2d, nw, w_kv, kn, cast, *, tm=256):
    t, d = mem2d.shape
    row = pl.BlockSpec((tm, d), lambda i: (i, 0))
    out = pl.BlockSpec((tm, MEM_Q_W), lambda i: (i, 0))
    cast_specs, cast_shapes = _cast_specs(cast, t // tm)
    outs = pl.pallas_call(
        functools.partial(_memkv_body, n_cast=len(cast)),
        out_shape=(jax.ShapeDtypeStruct((t, MEM_Q_W), BF16),) * 2 + tuple(cast_shapes),
        grid=(t // tm,),
        in_specs=[row, _resident(nw.shape), _resident(w_kv.shape), _resident(kn.shape), *cast_specs],
        out_specs=(out, out, *cast_specs),
        compiler_params=_params(("arbitrary",)),
        name="memkv",
    )(mem2d, nw, w_kv, kn, *cast)
    return outs[0], outs[1], outs[2:]


def _ret_body(q_ref, k_ref, v_ref, dmat_ref, qdec_ref, kdec_ref, cdec_ref, o_ref, state_ref, *, seq):
    dmat = dmat_ref[0]
    qdec = qdec_ref[0]
    kdec = kdec_ref[0]
    cdec = cdec_ref[0]

    def local(n):
        rows = slice(n * RET_BLOCK, (n + 1) * RET_BLOCK)
        q = q_ref[rows, :]
        k = k_ref[rows, :]
        v = v_ref[rows, :]
        s = _dot_nt(q, k) * dmat
        o = _dot(s.astype(BF16), v)
        upd = _dot_tn((k.astype(F32) * kdec).astype(BF16), v)
        qd = (q.astype(F32) * qdec).astype(BF16) if n else None
        return o, upd, qd

    nblk = seq // RET_BLOCK
    nxt = local(0)
    for n in range(nblk):
        rows = slice(n * RET_BLOCK, (n + 1) * RET_BLOCK)
        o, upd, qd = nxt
        if n + 1 < nblk:
            nxt = local(n + 1)
        if n == 0:
            state_ref[...] = upd
        else:
            st = state_ref[...]
            o = o + _dot(qd, st.astype(BF16))
            state_ref[...] = st * cdec + upd
        o_ref[rows, :] = _rms(o).astype(BF16)


def _retention(rq, rk, rv, dmat, qdec, kdec, cdec, *, batch, seq):
    qk = pl.BlockSpec((seq, RET_DK), lambda b, h: (b, h))
    vv = pl.BlockSpec((seq, RET_DV), lambda b, h: (b, h))

    def per_head(shape):
        return pl.BlockSpec((1,) + shape[1:], lambda b, h: (h, 0, 0))

    return pl.pallas_call(
        functools.partial(_ret_body, seq=seq),
        out_shape=jax.ShapeDtypeStruct(rv.shape, BF16),
        grid=(batch, RET_HEADS),
        in_specs=[qk, qk, vv, per_head(dmat.shape), per_head(qdec.shape), per_head(kdec.shape),
                  per_head(cdec.shape)],
        out_specs=vv,
        scratch_shapes=[pltpu.VMEM((RET_DK, RET_DV), F32)],
        compiler_params=_params(("parallel", "parallel")),
        name="retention",
    )(rq, rk, rv, dmat, qdec, kdec, cdec)


def _diff_body(q_ref, k_ref, v_ref, lamv_ref, subln_ref, o_ref, vx_ref, *, seq):
    lv = lamv_ref[...]
    lam = (jnp.exp(jnp.sum(lv[0:1] * lv[1:2], axis=-1, keepdims=True))
           - jnp.exp(jnp.sum(lv[2:3] * lv[3:4], axis=-1, keepdims=True)) + LAM_INIT)
    gain = subln_ref[...] * (1.0 - LAM_INIT)
    tq = DIFF_QBLOCK
    low = lax.broadcasted_iota(jnp.int32, (1, LANES), 1) < DIFF_DK
    rchunk = (lax.broadcasted_iota(jnp.int32, (2 * tq, tq), 0) % tq) // CHUNK
    cchunk = lax.broadcasted_iota(jnp.int32, (2 * tq, tq), 1) // CHUNK
    visible = cchunk <= rchunk
    zero = jnp.zeros((), BF16)

    nq = seq // tq
    for hd in range(DIFF_HEADS_PER_STEP):
        vx_ref[hd, :, 0:DIFF_DV] = v_ref[:, hd * DIFF_DV:(hd + 1) * DIFF_DV]
        vx_ref[hd, :, DIFF_DV:2 * DIFF_DV] = jnp.ones((seq, DIFF_DV), BF16)

    def scores(item):
        hd, qi = item
        cols = slice(hd * LANES, (hd + 1) * LANES)
        q = q_ref[qi * tq:(qi + 1) * tq, cols]
        q2 = jnp.concatenate([jnp.where(low, q, zero), jnp.where(low, zero, q)], axis=0)
        sd = jnp.where(visible, _dot_nt(q2, k_ref[qi * tq:(qi + 1) * tq, cols]), -jnp.inf)
        sp = _dot_nt(q2, k_ref[0:qi * tq, cols]) if qi else None
        return sd, sp

    items = [(hd, qi) for hd in range(DIFF_HEADS_PER_STEP) for qi in reversed(range(nq))]
    ahead = [scores(item) for item in items[:DIFF_LOOKAHEAD]]
    for step, (hd, qi) in enumerate(items):
        rows = slice(qi * tq, (qi + 1) * tq)
        past = qi * tq
        sd, sp = ahead.pop(0)
        if step + DIFF_LOOKAHEAD < len(items):
            ahead.append(scores(items[step + DIFF_LOOKAHEAD]))
        m = jnp.max(sd, axis=-1, keepdims=True)
        if past:
            m = jnp.maximum(m, jnp.max(sp, axis=-1, keepdims=True))
        ol = _dot(jnp.exp2(sd - m).astype(BF16), vx_ref[hd, rows, :])
        if past:
            ol = ol + _dot(jnp.exp2(sp - m).astype(BF16), vx_ref[hd, 0:past, :])
        o0 = ol[0:tq, 0:DIFF_DV] / ol[0:tq, DIFF_DV:2 * DIFF_DV]
        o1 = ol[tq:2 * tq, 0:DIFF_DV] / ol[tq:2 * tq, DIFF_DV:2 * DIFF_DV]
        o_ref[rows, hd * DIFF_DV:(hd + 1) * DIFF_DV] = (_rms(o0 - lam * o1) * gain).astype(BF16)


def _diff_attention(dq, dk, dv, lamv, subln, *, batch, seq):
    blk = pl.BlockSpec((seq, DIFF_HEADS_PER_STEP * LANES), lambda b, h: (b, h))
    return pl.pallas_call(
        functools.partial(_diff_body, seq=seq),
        out_shape=jax.ShapeDtypeStruct(dv.shape, BF16),
        grid=(batch, DIFF_HEADS // DIFF_HEADS_PER_STEP),
        in_specs=[blk, blk, blk, _resident(lamv.shape), _resident(subln.shape)],
        out_specs=blk,
        scratch_shapes=[pltpu.VMEM((DIFF_HEADS_PER_STEP, seq, 2 * DIFF_DV), BF16)],
        compiler_params=_params(("parallel", "parallel")),
        name="diff_attention",
    )(dq, dk, dv, lamv, subln)


def _merge_body(x_ref, ro_ref, do_ref, mq_ref, mk_ref, mv_ref, nw_ref, wrg_ref, wg0_ref, wg1_ref, wg2_ref, bg_ref,
                wro_ref, wdo_ref, wmo_ref, wout_ref, o_ref):
    x = x_ref[...]
    h = _rms(x, nw_ref[...]).astype(BF16)
    rg = _dot(h, wrg_ref[...])
    g = [jax.nn.sigmoid(_dot(h, w_ref[...]) + bg_ref[:, i * D_MODEL:(i + 1) * D_MODEL])
         for i, w_ref in enumerate((wg0_ref, wg1_ref, wg2_ref))]
    ro = (ro_ref[...].astype(F32) * (rg * jax.nn.sigmoid(rg))).astype(BF16)
    ret_out = _dot(ro, wro_ref[...])
    diff_out = _dot(do_ref[...], wdo_ref[...])
    mem_out = None
    for hd in range(MEM_HEADS):
        cols = slice(hd * MEM_DH, (hd + 1) * MEM_DH)
        s = _dot_nt(mq_ref[:, cols], mk_ref[:, cols])
        p = jnp.exp(s - jnp.max(s, axis=-1, keepdims=True))
        p = p * (1.0 / jnp.sum(p, axis=-1, keepdims=True))
        oh = _dot(p.astype(BF16), mv_ref[:, cols])
        part = _dot(oh.astype(BF16), wmo_ref[cols, :])
        mem_out = part if mem_out is None else mem_out + part
    merged = g[0] * ret_out + g[1] * diff_out + g[2] * mem_out
    o_ref[...] = x + _dot(merged.astype(BF16), wout_ref[...])


def _merge(x1, ro, do, mq, mk, mv, nw, w_in, bg, wro, wdo, wmo, wout, *, seq, mem_len, tm=512):
    t, d = x1.shape
    per_b = seq // tm

    def row(w):
        return pl.BlockSpec((tm, w), lambda i: (i, 0))

    memb = pl.BlockSpec((mem_len, MEM_Q_W), lambda i: (i // per_b, 0))
    return pl.pallas_call(
        _merge_body,
        out_shape=jax.ShapeDtypeStruct((t, d), F32),
        grid=(t // tm,),
        in_specs=[row(d), row(RET_V_W), row(DIFF_V_W), row(MEM_Q_W), memb, memb,
                  _resident(nw.shape), _resident_cols(d, RET_V_W, 2),
                  _resident_cols(d, D_MODEL, QKV_W // D_MODEL),
                  _resident_cols(d, D_MODEL, QKV_W // D_MODEL + 1),
                  _resident_cols(d, D_MODEL, QKV_W // D_MODEL + 2), _resident(bg.shape), _resident(wro.shape),
                  _resident(wdo.shape), _resident(wmo.shape), _resident(wout.shape)],
        out_specs=row(d),
        compiler_params=_params(("parallel",)),
        name="merge",
    )(x1, ro, do, mq, mk, mv, nw, w_in, w_in, w_in, w_in, bg, wro, wdo, wmo, wout)


def _deinterleave_matrix():
    j = jnp.arange(RET_DK)
    src = jnp.where(j < RET_DK // 2, 2 * j, 2 * (j - RET_DK // 2) + 1)
    return (jnp.arange(RET_DK)[:, None] == src[None, :]).astype(BF16)


def _rotation_tables(positions):
    pos = positions.astype(F32).reshape(-1, 1)
    ret_inv = (1.0 / (RET_THETA_BASE ** jnp.linspace(0.0, 1.0, RET_DK // 2, dtype=F32))).reshape(1, -1)
    rope_inv = 1.0 / (ROPE_THETA ** (jnp.arange(0, ROT_DIM, 2, dtype=F32) / ROT_DIM))
    d_ang = pos * rope_inv
    dtab = jnp.pad(jnp.concatenate([jnp.cos(d_ang), jnp.sin(d_ang)], axis=-1), ((0, 0), (0, LANES - ROT_DIM)))
    return pos, ret_inv, dtab


def _decay_tables():
    log_g = jnp.log(1.0 - 2.0 ** (-5.0 - jnp.arange(RET_HEADS, dtype=F32)))
    idx = jnp.arange(RET_BLOCK, dtype=F32)
    dist = jnp.abs(idx[:, None] - idx[None, :])
    chunk = jnp.arange(RET_BLOCK) // CHUNK
    visible = chunk[None, :] <= chunk[:, None]
    dmat = jnp.where(visible[None], jnp.exp(log_g[:, None, None] * dist[None]), 0.0)
    qdec = jnp.exp(log_g[:, None] * (idx[None, :] + 1.0))[..., None]
    kdec = jnp.exp(log_g[:, None] * (RET_BLOCK - 1.0 - idx[None, :]))[..., None]
    cdec = jnp.broadcast_to(jnp.exp(log_g * RET_BLOCK)[:, None, None], (RET_HEADS, 1, RET_DV))
    return dmat, qdec, kdec, cdec


def kernel(x, mem, positions, ffn1_norm, ffn1_w_gate, ffn1_w_up, ffn1_w_down, mix_norm, w_in, b_gate, ret_w_o, diff_q_norm, diff_k_norm, diff_lambda_q1, diff_lambda_k1, diff_lambda_q2, diff_lambda_k2, diff_subln, diff_w_o, mem_norm, mem_w_kv, mem_q_norm, mem_k_norm, mem_w_o, w_out, ffn2_norm, ffn2_w_gate, ffn2_w_up, ffn2_w_down, final_norm):
    batch, seq, d = x.shape
    mem_len = mem.shape[1]
    assert d == D_MODEL and seq % RET_BLOCK == 0 and seq % DIFF_QBLOCK == 0
    assert w_in.shape[0] == 1, "single-layer trunk"
    vec = lambda a: a.reshape(1, -1)

    pos, ret_inv, dtab = _rotation_tables(positions)
    dmat, qdec, kdec, cdec = _decay_tables()
    perm = _deinterleave_matrix()
    dqn = jnp.tile(vec(diff_q_norm[0]), (1, 2))
    dkn = jnp.tile(vec(diff_k_norm[0]), (1, 2))
    lamv = jnp.stack([diff_lambda_q1[0], diff_lambda_k1[0], diff_lambda_q2[0], diff_lambda_k2[0]]).astype(F32)

    mk, mv, (w1g, w1u, w1d) = _memkv(mem.reshape(batch * mem_len, d), vec(mem_norm[0]), mem_w_kv[0],
                                     vec(mem_k_norm[0]), (ffn1_w_gate[0], ffn1_w_up[0], ffn1_w_down[0]))
    x1, rcos, rsin, w_rqk, (wi, w_ro, w_do, w_mo, w_o) = _ffn_first(
        x.reshape(batch * seq, d), vec(ffn1_norm[0]), w1g, w1u, w1d, pos, ret_inv, perm,
        (w_in[0], ret_w_o[0], diff_w_o[0], mem_w_o[0], w_out[0]))
    (rq, rk, rv, dq, dk, dv, mq), (w2g, w2u, w2d) = _proj(
        x1, vec(mix_norm[0]), w_rqk, wi, rcos, rsin, dtab, dqn, dkn, vec(mem_q_norm[0]),
        (ffn2_w_gate[0], ffn2_w_up[0], ffn2_w_down[0]))
    ro = _retention(rq, rk, rv, dmat, qdec, kdec, cdec, batch=batch, seq=seq)
    do = _diff_attention(dq, dk, dv, lamv, vec(diff_subln[0]), batch=batch, seq=seq)
    x2 = _merge(x1, ro, do, mq, mk, mv, vec(mix_norm[0]), wi, vec(b_gate[0]), w_ro, w_do, w_mo, w_o,
                seq=seq, mem_len=mem_len)
    x3 = _ffn_last(x2, vec(ffn2_norm[0]), w2g, w2u, w2d, vec(final_norm[0]))
    return x3.reshape(batch, seq, d)
```

```python
import functools
import math

import jax
import jax.numpy as jnp
from jax import lax
from jax.experimental import pallas as pl
from jax.experimental.pallas import tpu as pltpu

D_MODEL = 1024
CHUNK = 64
EPS = 1e-6
RET_HEADS = 4
RET_DK = 256
RET_DV = 512
RET_THETA_BASE = 10000.0
DIFF_HEADS = 8
DIFF_DK = 64
DIFF_DV = 128
ROPE_THETA = 500000.0
ROT_DIM = 16
MEM_HEADS = 4
MEM_DH = 256
D_FF = 2816
LAM_INIT = 0.8 - 0.6 * math.exp(-0.3 * 0)
LOG2E = math.log2(math.e)

RET_QK_W = RET_HEADS * RET_DK
RET_V_W = RET_HEADS * RET_DV
DIFF_QK_W = DIFF_HEADS * 2 * DIFF_DK
DIFF_V_W = DIFF_HEADS * DIFF_DV
MEM_Q_W = MEM_HEADS * MEM_DH
QKV_W = 2 * RET_QK_W + 2 * RET_V_W + 2 * DIFF_QK_W + DIFF_V_W + MEM_Q_W
GATE_W = 3 * D_MODEL

LANES = 128
BF16_SUBLANES = 16
FFN_SUBROWS = 256
PROJ_SUBROWS = 256
MERGE_SUBROWS = 512
RET_BLOCK = 256
DIFF_QBLOCK = 256
DIFF_HEADS_PER_STEP = 2
DIFF_LOOKAHEAD = 1
VMEM_LIMIT = 56 * 1024 * 1024

BF16 = jnp.bfloat16
F32 = jnp.float32


def _dot(a, b):
    return jnp.dot(a, b, preferred_element_type=F32)


def _dot_nt(a, b):
    return lax.dot_general(a, b, (((1,), (1,)), ((), ())), preferred_element_type=F32)


def _dot_tn(a, b):
    return lax.dot_general(a, b, (((0,), (0,)), ((), ())), preferred_element_type=F32)


def _rms(x, w=None):
    y = x * lax.rsqrt(jnp.mean(x * x, axis=-1, keepdims=True) + EPS)
    if w is not None:
        y = y * w
    return y


def _resident(shape):
    nd = len(shape)
    return pl.BlockSpec(shape, lambda *_: (0,) * nd, pipeline_mode=pl.Buffered(1))


def _resident_cols(rows, width, index):
    return pl.BlockSpec((rows, width), lambda *_: (0, index), pipeline_mode=pl.Buffered(1))


def _params(sem):
    return pltpu.CompilerParams(dimension_semantics=sem, vmem_limit_bytes=VMEM_LIMIT)


def _swiglu_half_step(x, nw_ref, wg_ref, wu_ref, wd_ref):
    h = _rms(x, nw_ref[...]).astype(BF16)
    g = _dot(h, wg_ref[...])
    u = _dot(h, wu_ref[...])
    a = (g * jax.nn.sigmoid(g) * u).astype(BF16)
    return x + 0.5 * _dot(a, wd_ref[...])


def _cast_chunks(src_refs, dst_refs):
    for src, dst in zip(src_refs, dst_refs):
        dst[...] = src[...].astype(BF16)


def _cast_specs(mats, steps):
    specs, shapes = [], []
    for m in mats:
        rows, cols = m.shape
        chunk = next(c for c in range(BF16_SUBLANES, rows + 1, BF16_SUBLANES)
                     if rows % c == 0 and rows // c <= steps)
        last = rows // chunk - 1
        specs.append(pl.BlockSpec((chunk, cols), lambda i, last=last: (jnp.minimum(i, last), 0)))
        shapes.append(jax.ShapeDtypeStruct((rows, cols), BF16))
    return specs, shapes


def _ffn_first_body(x_ref, nw_ref, wg_ref, wu_ref, wd_ref, pos_ref, rinv_ref, perm_ref, *rest, n_cast):
    cast_in, (o_ref, rcos_ref, rsin_ref, wrqk_ref), cast_out = (
        rest[:n_cast], rest[n_cast:n_cast + 4], rest[n_cast + 4:])
    ang = pos_ref[...] * rinv_ref[...]
    rcos_ref[...] = jnp.cos(ang)
    rsin_ref[...] = jnp.sin(ang)
    _cast_chunks(cast_in, cast_out)
    perm = perm_ref[...]
    for hb in range(2 * RET_HEADS):
        cols = slice(hb * RET_DK, (hb + 1) * RET_DK)
        wrqk_ref[:, cols] = _dot(cast_in[0][:, cols].astype(BF16), perm).astype(BF16)
    for r in range(0, x_ref.shape[0], FFN_SUBROWS):
        rows = slice(r, r + FFN_SUBROWS)
        o_ref[rows, :] = _swiglu_half_step(x_ref[rows, :], nw_ref, wg_ref, wu_ref, wd_ref)


def _ffn_first(x, nw, wg, wu, wd, pos, rinv, perm, cast, *, tm=1024):
    t, d = x.shape
    steps = t // tm
    row = pl.BlockSpec((tm, d), lambda i: (i, 0))
    tab = pl.BlockSpec((tm, LANES), lambda i: (i, 0))
    cast_specs, cast_shapes = _cast_specs(cast, steps)
    w_in_chunk = cast_specs[0].block_shape[0]
    assert cast[0].shape[0] == w_in_chunk * steps, "every step must see a fresh row chunk of w_in"
    wrqk_spec = pl.BlockSpec((w_in_chunk, 2 * RET_QK_W), lambda i: (i, 0))
    outs = pl.pallas_call(
        functools.partial(_ffn_first_body, n_cast=len(cast)),
        out_shape=(jax.ShapeDtypeStruct((t, d), F32), jax.ShapeDtypeStruct((t, LANES), F32),
                   jax.ShapeDtypeStruct((t, LANES), F32),
                   jax.ShapeDtypeStruct((cast[0].shape[0], 2 * RET_QK_W), BF16), *cast_shapes),
        grid=(steps,),
        in_specs=[row, _resident(nw.shape), _resident(wg.shape), _resident(wu.shape), _resident(wd.shape),
                  pl.BlockSpec((tm, 1), lambda i: (i, 0)), _resident(rinv.shape), _resident(perm.shape),
                  *cast_specs],
        out_specs=(row, tab, tab, wrqk_spec, *cast_specs),
        compiler_params=_params(("arbitrary",)),
        name="ffn_first",
    )(x, nw, wg, wu, wd, pos, rinv, perm, *cast)
    return outs[0], outs[1], outs[2], outs[3], outs[4:]


def _ffn_last_body(x_ref, nw_ref, wg_ref, wu_ref, wd_ref, fw_ref, o_ref):
    for r in range(0, x_ref.shape[0], FFN_SUBROWS):
        rows = slice(r, r + FFN_SUBROWS)
        o_ref[rows, :] = _rms(_swiglu_half_step(x_ref[rows, :], nw_ref, wg_ref, wu_ref, wd_ref), fw_ref[...])


def _ffn_last(x, nw, wg, wu, wd, fw, *, tm=1024):
    t, d = x.shape
    row = pl.BlockSpec((tm, d), lambda i: (i, 0))
    return pl.pallas_call(
        _ffn_last_body,
        out_shape=jax.ShapeDtypeStruct((t, d), F32),
        grid=(t // tm,),
        in_specs=[row, _resident(nw.shape), _resident(wg.shape), _resident(wu.shape),
                  _resident(wd.shape), _resident(fw.shape)],
        out_specs=row,
        compiler_params=_params(("parallel",)),
        name="ffn_last",
    )(x, nw, wg, wu, wd, fw)


def _proj_body(x_ref, nw_ref, wrqk_ref, wrv_ref, wdq_ref, wdk_ref, wdv_ref, wmq_ref,
               rcos_ref, rsin_ref, dtab_ref, dqn_ref, dkn_ref, mqn_ref, *rest, n_cast):
    cast_in, cast_out = rest[:n_cast], rest[n_cast + 7:]
    rq_ref, rk_ref, rv_ref, dq_ref, dk_ref, dv_ref, mq_ref = rest[n_cast:n_cast + 7]
    lane = lax.broadcasted_iota(jnp.int32, (1, LANES), 1)
    half = ROT_DIM // 2
    low = lane < DIFF_DK
    first = (lane % DIFF_DK) < half
    for r in range(0, x_ref.shape[0], PROJ_SUBROWS):
        _proj_rows(slice(r, r + PROJ_SUBROWS), lane, half, low, first, x_ref, nw_ref, wrqk_ref, wrv_ref, wdq_ref,
                   wdk_ref, wdv_ref, wmq_ref, rcos_ref, rsin_ref, dtab_ref, dqn_ref, dkn_ref, mqn_ref,
                   rq_ref, rk_ref, rv_ref, dq_ref, dk_ref, dv_ref, mq_ref)
    _cast_chunks(cast_in, cast_out)


def _proj_rows(rows, lane, half, low, first, x_ref, nw_ref, wrqk_ref, wrv_ref, wdq_ref, wdk_ref, wdv_ref, wmq_ref,
               rcos_ref, rsin_ref, dtab_ref, dqn_ref, dkn_ref, mqn_ref,
               rq_ref, rk_ref, rv_ref, dq_ref, dk_ref, dv_ref, mq_ref):
    h = _rms(x_ref[rows, :], nw_ref[...]).astype(BF16)
    rcos = rcos_ref[rows, :]
    rsin = rsin_ref[rows, :]

    dtab = dtab_ref[:, rows].T
    c = jnp.where(lane < half, dtab, 0.0)
    c = c + pltpu.roll(c, half, 1)
    c = c + pltpu.roll(c, DIFF_DK, 1)
    dcos = jnp.where((lane % DIFF_DK) < ROT_DIM, c, 1.0)
    sn = jnp.where((lane >= half) & (lane < ROT_DIM), dtab, 0.0)
    sn = sn - pltpu.roll(sn, LANES - half, 1)
    dsin = sn + pltpu.roll(sn, DIFF_DK, 1)

    def ret_rotate(out_ref, scale):
        def epilogue(y):
            for hd in range(RET_HEADS):
                ye = y[:, hd * RET_DK:hd * RET_DK + LANES]
                yo = y[:, hd * RET_DK + LANES:(hd + 1) * RET_DK]
                out_ref[rows, hd * RET_DK:hd * RET_DK + LANES] = ((ye * rcos - yo * rsin) * scale).astype(BF16)
                out_ref[rows, hd * RET_DK + LANES:(hd + 1) * RET_DK] = ((yo * rcos + ye * rsin) * scale).astype(BF16)
        return epilogue

    def store(out_ref):
        def epilogue(y):
            out_ref[rows, :] = y.astype(BF16)
        return epilogue

    def diff_norm_rope(out_ref, n_ref, scale):
        def epilogue(y):
            gain = n_ref[...]
            for hd in range(DIFF_HEADS):
                yh = y[:, hd * LANES:(hd + 1) * LANES]
                sq = yh * yh
                ss_lo = jnp.sum(jnp.where(low, sq, 0.0), axis=-1, keepdims=True)
                ss_hi = jnp.sum(jnp.where(low, 0.0, sq), axis=-1, keepdims=True)
                r = jnp.where(low, lax.rsqrt(ss_lo * (1.0 / DIFF_DK) + EPS),
                              lax.rsqrt(ss_hi * (1.0 / DIFF_DK) + EPS))
                yn = yh * r * gain
                partner = jnp.where(first, pltpu.roll(yn, LANES - half, 1), pltpu.roll(yn, half, 1))
                out_ref[rows, hd * LANES:(hd + 1) * LANES] = ((yn * dcos + partner * dsin) * scale).astype(BF16)
        return epilogue

    def mem_norm(y):
        gain = mqn_ref[...]
        for hd in range(MEM_HEADS):
            yh = y[:, hd * MEM_DH:(hd + 1) * MEM_DH]
            mq_ref[rows, hd * MEM_DH:(hd + 1) * MEM_DH] = (_rms(yh) * gain).astype(BF16)

    stages = (
        (lambda: wrqk_ref[:, 0:RET_QK_W], ret_rotate(rq_ref, 1.0)),
        (lambda: wrv_ref[...], store(rv_ref)),
        (lambda: wrqk_ref[:, RET_QK_W:2 * RET_QK_W], ret_rotate(rk_ref, RET_DK ** -0.5)),
        (lambda: wdq_ref[...], diff_norm_rope(dq_ref, dqn_ref, LOG2E * DIFF_DK ** -0.5)),
        (lambda: wmq_ref[...], mem_norm),
        (lambda: wdk_ref[...], diff_norm_rope(dk_ref, dkn_ref, 1.0)),
        (lambda: wdv_ref[...], store(dv_ref)),
    )
    y_next = _dot(h, stages[0][0]())
    for i, (_, epilogue) in enumerate(stages):
        y = y_next
        if i + 1 < len(stages):
            y_next = _dot(h, stages[i + 1][0]())
        epilogue(y)


def _proj(x1, nw, w_rqk, w_in, rcos, rsin, dtab, dqn, dkn, mqn, cast, *, tm=512):
    t, d = x1.shape
    cast_specs, cast_shapes = _cast_specs(cast, t // tm)
    widths = (RET_QK_W, RET_QK_W, RET_V_W, DIFF_QK_W, DIFF_QK_W, DIFF_V_W, MEM_Q_W)

    def row(w):
        return pl.BlockSpec((tm, w), lambda i: (i, 0))

    outs = pl.pallas_call(
        functools.partial(_proj_body, n_cast=len(cast)),
        out_shape=(*(jax.ShapeDtypeStruct((t, w), BF16) for w in widths), *cast_shapes),
        grid=(t // tm,),
        in_specs=[row(d), _resident(nw.shape), _resident(w_rqk.shape),
                  _resident_cols(d, RET_V_W, 1),
                  _resident_cols(d, DIFF_QK_W, 6), _resident_cols(d, DIFF_QK_W, 7),
                  _resident_cols(d, DIFF_V_W, 8), _resident_cols(d, MEM_Q_W, 9),
                  row(LANES), row(LANES), pl.BlockSpec((LANES, tm), lambda i: (0, i)),
                  _resident(dqn.shape), _resident(dkn.shape), _resident(mqn.shape), *cast_specs],
        out_specs=(*(row(w) for w in widths), *cast_specs),
        compiler_params=_params(("arbitrary",)),
        name="proj",
    )(x1, nw, w_rqk, w_in, w_in, w_in, w_in, w_in, rcos, rsin, dtab, dqn, dkn, mqn, *cast)
    return outs[:len(widths)], outs[len(widths):]


def _memkv_body(m_ref, nw_ref, w_ref, kn_ref, *rest, n_cast):
    cast_in, (mk_ref, mv_ref), cast_out = rest[:n_cast], rest[n_cast:n_cast + 2], rest[n_cast + 2:]
    _cast_chunks(cast_in, cast_out)
    hm = _rms(m_ref[...], nw_ref[...]).astype(BF16)
    k = _dot(hm, w_ref[:, 0:MEM_Q_W].astype(BF16))
    gain = kn_ref[...] * (MEM_DH ** -0.5)
    for hd in range(MEM_HEADS):
        kh = k[:, hd * MEM_DH:(hd + 1) * MEM_DH]
        mk_ref[:, hd * MEM_DH:(hd + 1) * MEM_DH] = (_rms(kh) * gain).astype(BF16)
    mv_ref[...] = _dot(hm, w_ref[:, MEM_Q_W:2 * MEM_Q_W].astype(BF16)).astype(BF16)


def _memkv(mem2d, nw, w_kv, kn, cast, *, tm=256):
    t, d = mem2d.shape
    row = pl.BlockSpec((tm, d), lambda i: (i, 0))
    out = pl.BlockSpec((tm, MEM_Q_W), lambda i: (i, 0))
    cast_specs, cast_shapes = _cast_specs(cast, t // tm)
    outs = pl.pallas_call(
        functools.partial(_memkv_body, n_cast=len(cast)),
        out_shape=(jax.ShapeDtypeStruct((t, MEM_Q_W), BF16),) * 2 + tuple(cast_shapes),
        grid=(t // tm,),
        in_specs=[row, _resident(nw.shape), _resident(w_kv.shape), _resident(kn.shape), *cast_specs],
        out_specs=(out, out, *cast_specs),
        compiler_params=_params(("arbitrary",)),
        name="memkv",
    )(mem2d, nw, w_kv, kn, *cast)
    return outs[0], outs[1], outs[2:]


def _ret_body(q_ref, k_ref, v_ref, dmat_ref, qdec_ref, kdec_ref, cdec_ref, o_ref, state_ref, *, seq):
    dmat = dmat_ref[0]
    qdec = qdec_ref[0]
    kdec = kdec_ref[0]
    cdec = cdec_ref[0]

    def local(n):
        rows = slice(n * RET_BLOCK, (n + 1) * RET_BLOCK)
        q = q_ref[rows, :]
        k = k_ref[rows, :]
        v = v_ref[rows, :]
        s = _dot_nt(q, k) * dmat
        o = _dot(s.astype(BF16), v)
        upd = _dot_tn((k.astype(F32) * kdec).astype(BF16), v)
        qd = (q.astype(F32) * qdec).astype(BF16) if n else None
        return o, upd, qd

    nblk = seq // RET_BLOCK
    nxt = local(0)
    for n in range(nblk):
        rows = slice(n * RET_BLOCK, (n + 1) * RET_BLOCK)
        o, upd, qd = nxt
        if n + 1 < nblk:
            nxt = local(n + 1)
        if n == 0:
            state_ref[...] = upd
        else:
            st = state_ref[...]
            o = o + _dot(qd, st.astype(BF16))
            state_ref[...] = st * cdec + upd
        o_ref[rows, :] = _rms(o).astype(BF16)


def _retention(rq, rk, rv, dmat, qdec, kdec, cdec, *, batch, seq):
    qk = pl.BlockSpec((seq, RET_DK), lambda b, h: (b, h))
    vv = pl.BlockSpec((seq, RET_DV), lambda b, h: (b, h))

    def per_head(shape):
        return pl.BlockSpec((1,) + shape[1:], lambda b, h: (h, 0, 0))

    return pl.pallas_call(
        functools.partial(_ret_body, seq=seq),
        out_shape=jax.ShapeDtypeStruct(rv.shape, BF16),
        grid=(batch, RET_HEADS),
        in_specs=[qk, qk, vv, per_head(dmat.shape), per_head(qdec.shape), per_head(kdec.shape),
                  per_head(cdec.shape)],
        out_specs=vv,
        scratch_shapes=[pltpu.VMEM((RET_DK, RET_DV), F32)],
        compiler_params=_params(("parallel", "parallel")),
        name="retention",
    )(rq, rk, rv, dmat, qdec, kdec, cdec)


def _diff_body(q_ref, k_ref, v_ref, lamv_ref, subln_ref, o_ref, vx_ref, *, seq):
    lv = lamv_ref[...]
    lam = (jnp.exp(jnp.sum(lv[0:1] * lv[1:2], axis=-1, keepdims=True))
           - jnp.exp(jnp.sum(lv[2:3] * lv[3:4], axis=-1, keepdims=True)) + LAM_INIT)
    gain = subln_ref[...] * (1.0 - LAM_INIT)
    tq = DIFF_QBLOCK
    low = lax.broadcasted_iota(jnp.int32, (1, LANES), 1) < DIFF_DK
    rchunk = (lax.broadcasted_iota(jnp.int32, (2 * tq, tq), 0) % tq) // CHUNK
    cchunk = lax.broadcasted_iota(jnp.int32, (2 * tq, tq), 1) // CHUNK
    visible = cchunk <= rchunk
    zero = jnp.zeros((), BF16)

    nq = seq // tq
    for hd in range(DIFF_HEADS_PER_STEP):
        vx_ref[hd, :, 0:DIFF_DV] = v_ref[:, hd * DIFF_DV:(hd + 1) * DIFF_DV]
        vx_ref[hd, :, DIFF_DV:2 * DIFF_DV] = jnp.ones((seq, DIFF_DV), BF16)

    def scores(item):
        hd, qi = item
        cols = slice(hd * LANES, (hd + 1) * LANES)
        q = q_ref[qi * tq:(qi + 1) * tq, cols]
        q2 = jnp.concatenate([jnp.where(low, q, zero), jnp.where(low, zero, q)], axis=0)
        sd = jnp.where(visible, _dot_nt(q2, k_ref[qi * tq:(qi + 1) * tq, cols]), -jnp.inf)
        sp = _dot_nt(q2, k_ref[0:qi * tq, cols]) if qi else None
        return sd, sp

    items = [(hd, qi) for hd in range(DIFF_HEADS_PER_STEP) for qi in reversed(range(nq))]
    ahead = [scores(item) for item in items[:DIFF_LOOKAHEAD]]
    for step, (hd, qi) in enumerate(items):
        rows = slice(qi * tq, (qi + 1) * tq)
        past = qi * tq
        sd, sp = ahead.pop(0)
        if step + DIFF_LOOKAHEAD < len(items):
            ahead.append(scores(items[step + DIFF_LOOKAHEAD]))
        m = jnp.max(sd, axis=-1, keepdims=True)
        if past:
            m = jnp.maximum(m, jnp.max(sp, axis=-1, keepdims=True))
        ol = _dot(jnp.exp2(sd - m).astype(BF16), vx_ref[hd, rows, :])
        if past:
            ol = ol + _dot(jnp.exp2(sp - m).astype(BF16), vx_ref[hd, 0:past, :])
        o0 = ol[0:tq, 0:DIFF_DV] / ol[0:tq, DIFF_DV:2 * DIFF_DV]
        o1 = ol[tq:2 * tq, 0:DIFF_DV] / ol[tq:2 * tq, DIFF_DV:2 * DIFF_DV]
        o_ref[rows, hd * DIFF_DV:(hd + 1) * DIFF_DV] = (_rms(o0 - lam * o1) * gain).astype(BF16)


def _diff_attention(dq, dk, dv, lamv, subln, *, batch, seq):
    blk = pl.BlockSpec((seq, DIFF_HEADS_PER_STEP * LANES), lambda b, h: (b, h))
    return pl.pallas_call(
        functools.partial(_diff_body, seq=seq),
        out_shape=jax.ShapeDtypeStruct(dv.shape, BF16),
        grid=(batch, DIFF_HEADS // DIFF_HEADS_PER_STEP),
        in_specs=[blk, blk, blk, _resident(lamv.shape), _resident(subln.shape)],
        out_specs=blk,
        scratch_shapes=[pltpu.VMEM((DIFF_HEADS_PER_STEP, seq, 2 * DIFF_DV), BF16)],
        compiler_params=_params(("parallel", "parallel")),
        name="diff_attention",
    )(dq, dk, dv, lamv, subln)


def _merge_body(x_ref, ro_ref, do_ref, mq_ref, mk_ref, mv_ref, nw_ref, wrg_ref, wg0_ref, wg1_ref, wg2_ref, bg_ref,
                wro_ref, wdo_ref, wmo_ref, wout_ref, o_ref):
    for r in range(0, x_ref.shape[0], MERGE_SUBROWS):
        rows = slice(r, r + MERGE_SUBROWS)
        x = x_ref[rows, :]
        h = _rms(x, nw_ref[...]).astype(BF16)
        rg = _dot(h, wrg_ref[...])
        g = [jax.nn.sigmoid(_dot(h, w_ref[...]) + bg_ref[:, i * D_MODEL:(i + 1) * D_MODEL])
             for i, w_ref in enumerate((wg0_ref, wg1_ref, wg2_ref))]
        ro = (ro_ref[rows, :].astype(F32) * (rg * jax.nn.sigmoid(rg))).astype(BF16)
        ret_out = _dot(ro, wro_ref[...])
        diff_out = _dot(do_ref[rows, :], wdo_ref[...])
        mem_out = None
        for hd in range(MEM_HEADS):
            cols = slice(hd * MEM_DH, (hd + 1) * MEM_DH)
            s = _dot_nt(mq_ref[rows, cols], mk_ref[:, cols])
            p = jnp.exp(s - jnp.max(s, axis=-1, keepdims=True))
            p = p * (1.0 / jnp.sum(p, axis=-1, keepdims=True))
            oh = _dot(p.astype(BF16), mv_ref[:, cols])
            part = _dot(oh.astype(BF16), wmo_ref[cols, :])
            mem_out = part if mem_out is None else mem_out + part
        merged = g[0] * ret_out + g[1] * diff_out + g[2] * mem_out
        o_ref[rows, :] = x + _dot(merged.astype(BF16), wout_ref[...])


def _merge(x1, ro, do, mq, mk, mv, nw, w_in, bg, wro, wdo, wmo, wout, *, seq, mem_len, tm=512):
    t, d = x1.shape
    per_b = seq // tm

    def row(w):
        return pl.BlockSpec((tm, w), lambda i: (i, 0))

    memb = pl.BlockSpec((mem_len, MEM_Q_W), lambda i: (i // per_b, 0))
    return pl.pallas_call(
        _merge_body,
        out_shape=jax.ShapeDtypeStruct((t, d), F32),
        grid=(t // tm,),
        in_specs=[row(d), row(RET_V_W), row(DIFF_V_W), row(MEM_Q_W), memb, memb,
                  _resident(nw.shape), _resident_cols(d, RET_V_W, 2),
                  _resident_cols(d, D_MODEL, QKV_W // D_MODEL),
                  _resident_cols(d, D_MODEL, QKV_W // D_MODEL + 1),
                  _resident_cols(d, D_MODEL, QKV_W // D_MODEL + 2), _resident(bg.shape), _resident(wro.shape),
                  _resident(wdo.shape), _resident(wmo.shape), _resident(wout.shape)],
        out_specs=row(d),
        compiler_params=_params(("parallel",)),
        name="merge",
    )(x1, ro, do, mq, mk, mv, nw, w_in, w_in, w_in, w_in, bg, wro, wdo, wmo, wout)


def _deinterleave_matrix():
    j = jnp.arange(RET_DK)
    src = jnp.where(j < RET_DK // 2, 2 * j, 2 * (j - RET_DK // 2) + 1)
    return (jnp.arange(RET_DK)[:, None] == src[None, :]).astype(BF16)


def _rotation_tables(positions):
    pos = positions.astype(F32).reshape(-1, 1)
    ret_inv = (1.0 / (RET_THETA_BASE ** jnp.linspace(0.0, 1.0, RET_DK // 2, dtype=F32))).reshape(1, -1)
    rope_inv = 1.0 / (ROPE_THETA ** (jnp.arange(0, ROT_DIM, 2, dtype=F32) / ROT_DIM))
    d_ang = rope_inv[:, None] * pos.reshape(1, -1)
    dtab = jnp.concatenate([jnp.cos(d_ang), jnp.sin(d_ang), jnp.zeros((LANES - ROT_DIM, pos.shape[0]), F32)], axis=0)
    return pos, ret_inv, dtab


def _decay_tables():
    log_g = jnp.log(1.0 - 2.0 ** (-5.0 - jnp.arange(RET_HEADS, dtype=F32)))
    idx = jnp.arange(RET_BLOCK, dtype=F32)
    dist = jnp.abs(idx[:, None] - idx[None, :])
    chunk = jnp.arange(RET_BLOCK) // CHUNK
    visible = chunk[None, :] <= chunk[:, None]
    dmat = jnp.where(visible[None], jnp.exp(log_g[:, None, None] * dist[None]), 0.0)
    qdec = jnp.exp(log_g[:, None] * (idx[None, :] + 1.0))[..., None]
    kdec = jnp.exp(log_g[:, None] * (RET_BLOCK - 1.0 - idx[None, :]))[..., None]
    cdec = jnp.broadcast_to(jnp.exp(log_g * RET_BLOCK)[:, None, None], (RET_HEADS, 1, RET_DV))
    return dmat, qdec, kdec, cdec


def kernel(x, mem, positions, ffn1_norm, ffn1_w_gate, ffn1_w_up, ffn1_w_down, mix_norm, w_in, b_gate, ret_w_o, diff_q_norm, diff_k_norm, diff_lambda_q1, diff_lambda_k1, diff_lambda_q2, diff_lambda_k2, diff_subln, diff_w_o, mem_norm, mem_w_kv, mem_q_norm, mem_k_norm, mem_w_o, w_out, ffn2_norm, ffn2_w_gate, ffn2_w_up, ffn2_w_down, final_norm):
    batch, seq, d = x.shape
    mem_len = mem.shape[1]
    assert d == D_MODEL and seq % RET_BLOCK == 0 and seq % DIFF_QBLOCK == 0
    assert w_in.shape[0] == 1, "single-layer trunk"
    vec = lambda a: a.reshape(1, -1)

    pos, ret_inv, dtab = _rotation_tables(positions)
    dmat, qdec, kdec, cdec = _decay_tables()
    perm = _deinterleave_matrix()
    dqn = jnp.tile(vec(diff_q_norm[0]), (1, 2))
    dkn = jnp.tile(vec(diff_k_norm[0]), (1, 2))
    lamv = jnp.stack([diff_lambda_q1[0], diff_lambda_k1[0], diff_lambda_q2[0], diff_lambda_k2[0]]).astype(F32)

    mk, mv, (w1g, w1u, w1d) = _memkv(mem.reshape(batch * mem_len, d), vec(mem_norm[0]), mem_w_kv[0],
                                     vec(mem_k_norm[0]), (ffn1_w_gate[0], ffn1_w_up[0], ffn1_w_down[0]))
    x1, rcos, rsin, w_rqk, (wi, w_ro, w_do, w_mo, w_o) = _ffn_first(
        x.reshape(batch * seq, d), vec(ffn1_norm[0]), w1g, w1u, w1d, pos, ret_inv, perm,
        (w_in[0], ret_w_o[0], diff_w_o[0], mem_w_o[0], w_out[0]))
    (rq, rk, rv, dq, dk, dv, mq), (w2g, w2u, w2d) = _proj(
        x1, vec(mix_norm[0]), w_rqk, wi, rcos, rsin, dtab, dqn, dkn, vec(mem_q_norm[0]),
        (ffn2_w_gate[0], ffn2_w_up[0], ffn2_w_down[0]))
    ro = _retention(rq, rk, rv, dmat, qdec, kdec, cdec, batch=batch, seq=seq)
    do = _diff_attention(dq, dk, dv, lamv, vec(diff_subln[0]), batch=batch, seq=seq)
    x2 = _merge(x1, ro, do, mq, mk, mv, vec(mix_norm[0]), wi, vec(b_gate[0]), w_ro, w_do, w_mo, w_o,
                seq=seq, mem_len=mem_len)
    x3 = _ffn_last(x2, vec(ffn2_norm[0]), w2g, w2u, w2d, vec(final_norm[0]))
    return x3.reshape(batch, seq, d)
```

```python
import functools
import math

import jax
import jax.numpy as jnp
from jax import lax
from jax.experimental import pallas as pl
from jax.experimental.pallas import tpu as pltpu

D_MODEL = 1024
CHUNK = 64
EPS = 1e-6
RET_HEADS = 4
RET_DK = 256
RET_DV = 512
RET_THETA_BASE = 10000.0
DIFF_HEADS = 8
DIFF_DK = 64
DIFF_DV = 128
ROPE_THETA = 500000.0
ROT_DIM = 16
MEM_HEADS = 4
MEM_DH = 256
D_FF = 2816
LAM_INIT = 0.8 - 0.6 * math.exp(-0.3 * 0)
LOG2E = math.log2(math.e)

RET_QK_W = RET_HEADS * RET_DK
RET_V_W = RET_HEADS * RET_DV
DIFF_QK_W = DIFF_HEADS * 2 * DIFF_DK
DIFF_V_W = DIFF_HEADS * DIFF_DV
MEM_Q_W = MEM_HEADS * MEM_DH
QKV_W = 2 * RET_QK_W + 2 * RET_V_W + 2 * DIFF_QK_W + DIFF_V_W + MEM_Q_W
GATE_W = 3 * D_MODEL

LANES = 128
BF16_SUBLANES = 16
FFN_SUBROWS = 256
PROJ_SUBROWS = 256
MERGE_SUBROWS = 256
RET_BLOCK = 256
DIFF_QBLOCK = 256
DIFF_HEADS_PER_STEP = 2
DIFF_LOOKAHEAD = 1
VMEM_LIMIT = 56 * 1024 * 1024

BF16 = jnp.bfloat16
F32 = jnp.float32


def _dot(a, b):
    return jnp.dot(a, b, preferred_element_type=F32)


def _dot_nt(a, b):
    return lax.dot_general(a, b, (((1,), (1,)), ((), ())), preferred_element_type=F32)


def _dot_tn(a, b):
    return lax.dot_general(a, b, (((0,), (0,)), ((), ())), preferred_element_type=F32)


def _rms(x, w=None):
    y = x * lax.rsqrt(jnp.mean(x * x, axis=-1, keepdims=True) + EPS)
    if w is not None:
        y = y * w
    return y


def _resident(shape):
    nd = len(shape)
    return pl.BlockSpec(shape, lambda *_: (0,) * nd, pipeline_mode=pl.Buffered(1))


def _resident_cols(rows, width, index):
    return pl.BlockSpec((rows, width), lambda *_: (0, index), pipeline_mode=pl.Buffered(1))


def _params(sem):
    return pltpu.CompilerParams(dimension_semantics=sem, vmem_limit_bytes=VMEM_LIMIT)


def _swiglu_half_step(x, nw_ref, wg_ref, wu_ref, wd_ref):
    h = _rms(x, nw_ref[...]).astype(BF16)
    g = _dot(h, wg_ref[...])
    u = _dot(h, wu_ref[...])
    a = (g * jax.nn.sigmoid(g) * u).astype(BF16)
    return x + 0.5 * _dot(a, wd_ref[...])


def _cast_chunks(src_refs, dst_refs):
    for src, dst in zip(src_refs, dst_refs):
        dst[...] = src[...].astype(BF16)


def _cast_specs(mats, steps):
    specs, shapes = [], []
    for m in mats:
        rows, cols = m.shape
        chunk = next(c for c in range(BF16_SUBLANES, rows + 1, BF16_SUBLANES)
                     if rows % c == 0 and rows // c <= steps)
        last = rows // chunk - 1
        specs.append(pl.BlockSpec((chunk, cols), lambda i, last=last: (jnp.minimum(i, last), 0)))
        shapes.append(jax.ShapeDtypeStruct((rows, cols), BF16))
    return specs, shapes


def _ffn_first_body(x_ref, nw_ref, wg_ref, wu_ref, wd_ref, perm_ref, *rest, n_cast):
    cast_in, (o_ref, wrqk_ref), cast_out = rest[:n_cast], rest[n_cast:n_cast + 2], rest[n_cast + 2:]
    _cast_chunks(cast_in, cast_out)
    perm = perm_ref[...]
    for hb in range(2 * RET_HEADS):
        cols = slice(hb * RET_DK, (hb + 1) * RET_DK)
        wrqk_ref[:, cols] = _dot(cast_in[0][:, cols].astype(BF16), perm).astype(BF16)
    for r in range(0, x_ref.shape[0], FFN_SUBROWS):
        rows = slice(r, r + FFN_SUBROWS)
        o_ref[rows, :] = _swiglu_half_step(x_ref[rows, :], nw_ref, wg_ref, wu_ref, wd_ref)


def _ffn_first(x, nw, wg, wu, wd, perm, cast, *, tm=1024):
    t, d = x.shape
    steps = t // tm
    row = pl.BlockSpec((tm, d), lambda i: (i, 0))
    cast_specs, cast_shapes = _cast_specs(cast, steps)
    w_in_chunk = cast_specs[0].block_shape[0]
    assert cast[0].shape[0] == w_in_chunk * steps, "every step must see a fresh row chunk of w_in"
    wrqk_spec = pl.BlockSpec((w_in_chunk, 2 * RET_QK_W), lambda i: (i, 0))
    outs = pl.pallas_call(
        functools.partial(_ffn_first_body, n_cast=len(cast)),
        out_shape=(jax.ShapeDtypeStruct((t, d), F32),
                   jax.ShapeDtypeStruct((cast[0].shape[0], 2 * RET_QK_W), BF16), *cast_shapes),
        grid=(steps,),
        in_specs=[row, _resident(nw.shape), _resident(wg.shape), _resident(wu.shape), _resident(wd.shape),
                  _resident(perm.shape), *cast_specs],
        out_specs=(row, wrqk_spec, *cast_specs),
        compiler_params=_params(("arbitrary",)),
        name="ffn_first",
    )(x, nw, wg, wu, wd, perm, *cast)
    return outs[0], outs[1], outs[2:]


def _ffn_last_body(x_ref, nw_ref, wg_ref, wu_ref, wd_ref, fw_ref, o_ref):
    for r in range(0, x_ref.shape[0], FFN_SUBROWS):
        rows = slice(r, r + FFN_SUBROWS)
        o_ref[rows, :] = _rms(_swiglu_half_step(x_ref[rows, :], nw_ref, wg_ref, wu_ref, wd_ref), fw_ref[...])


def _ffn_last(x, nw, wg, wu, wd, fw, *, tm=1024):
    t, d = x.shape
    row = pl.BlockSpec((tm, d), lambda i: (i, 0))
    return pl.pallas_call(
        _ffn_last_body,
        out_shape=jax.ShapeDtypeStruct((t, d), F32),
        grid=(t // tm,),
        in_specs=[row, _resident(nw.shape), _resident(wg.shape), _resident(wu.shape),
                  _resident(wd.shape), _resident(fw.shape)],
        out_specs=row,
        compiler_params=_params(("parallel",)),
        name="ffn_last",
    )(x, nw, wg, wu, wd, fw)


def _proj_body(x_ref, nw_ref, wrqk_ref, wrv_ref, wdq_ref, wdk_ref, wdv_ref, wmq_ref,
               rcos_ref, rsin_ref, dtab_ref, dqn_ref, dkn_ref, mqn_ref, *rest, n_cast):
    cast_in, cast_out = rest[:n_cast], rest[n_cast + 7:]
    rq_ref, rk_ref, rv_ref, dq_ref, dk_ref, dv_ref, mq_ref = rest[n_cast:n_cast + 7]
    lane = lax.broadcasted_iota(jnp.int32, (1, LANES), 1)
    half = ROT_DIM // 2
    low = lane < DIFF_DK
    first = (lane % DIFF_DK) < half
    for r in range(0, x_ref.shape[0], PROJ_SUBROWS):
        _proj_rows(slice(r, r + PROJ_SUBROWS), lane, half, low, first, x_ref, nw_ref, wrqk_ref, wrv_ref, wdq_ref,
                   wdk_ref, wdv_ref, wmq_ref, rcos_ref, rsin_ref, dtab_ref, dqn_ref, dkn_ref, mqn_ref,
                   rq_ref, rk_ref, rv_ref, dq_ref, dk_ref, dv_ref, mq_ref)
    _cast_chunks(cast_in, cast_out)


def _proj_rows(rows, lane, half, low, first, x_ref, nw_ref, wrqk_ref, wrv_ref, wdq_ref, wdk_ref, wdv_ref, wmq_ref,
               rcos_ref, rsin_ref, dtab_ref, dqn_ref, dkn_ref, mqn_ref,
               rq_ref, rk_ref, rv_ref, dq_ref, dk_ref, dv_ref, mq_ref):
    h = _rms(x_ref[rows, :], nw_ref[...]).astype(BF16)
    rcos = rcos_ref[rows, :]
    rsin = rsin_ref[rows, :]

    dtab = dtab_ref[:, rows].T
    c = jnp.where(lane < half, dtab, 0.0)
    c = c + pltpu.roll(c, half, 1)
    c = c + pltpu.roll(c, DIFF_DK, 1)
    dcos = jnp.where((lane % DIFF_DK) < ROT_DIM, c, 1.0)
    sn = jnp.where((lane >= half) & (lane < ROT_DIM), dtab, 0.0)
    sn = sn - pltpu.roll(sn, LANES - half, 1)
    dsin = sn + pltpu.roll(sn, DIFF_DK, 1)

    def ret_rotate(out_ref, scale):
        def epilogue(y):
            for hd in range(RET_HEADS):
                ye = y[:, hd * RET_DK:hd * RET_DK + LANES]
                yo = y[:, hd * RET_DK + LANES:(hd + 1) * RET_DK]
                out_ref[rows, hd * RET_DK:hd * RET_DK + LANES] = ((ye * rcos - yo * rsin) * scale).astype(BF16)
                out_ref[rows, hd * RET_DK + LANES:(hd + 1) * RET_DK] = ((yo * rcos + ye * rsin) * scale).astype(BF16)
        return epilogue

    def store(out_ref):
        def epilogue(y):
            out_ref[rows, :] = y.astype(BF16)
        return epilogue

    def diff_norm_rope(out_ref, n_ref, scale):
        def epilogue(y):
            gain = n_ref[...]
            for hd in range(DIFF_HEADS):
                yh = y[:, hd * LANES:(hd + 1) * LANES]
                sq = yh * yh
                ss_lo = jnp.sum(jnp.where(low, sq, 0.0), axis=-1, keepdims=True)
                ss_hi = jnp.sum(jnp.where(low, 0.0, sq), axis=-1, keepdims=True)
                r = jnp.where(low, lax.rsqrt(ss_lo * (1.0 / DIFF_DK) + EPS),
                              lax.rsqrt(ss_hi * (1.0 / DIFF_DK) + EPS))
                yn = yh * r * gain
                partner = jnp.where(first, pltpu.roll(yn, LANES - half, 1), pltpu.roll(yn, half, 1))
                out_ref[rows, hd * LANES:(hd + 1) * LANES] = ((yn * dcos + partner * dsin) * scale).astype(BF16)
        return epilogue

    def mem_norm(y):
        gain = mqn_ref[...]
        for hd in range(MEM_HEADS):
            yh = y[:, hd * MEM_DH:(hd + 1) * MEM_DH]
            mq_ref[rows, hd * MEM_DH:(hd + 1) * MEM_DH] = (_rms(yh) * gain).astype(BF16)

    stages = (
        (lambda: wrqk_ref[:, 0:RET_QK_W], ret_rotate(rq_ref, 1.0)),
        (lambda: wrv_ref[...], store(rv_ref)),
        (lambda: wrqk_ref[:, RET_QK_W:2 * RET_QK_W], ret_rotate(rk_ref, RET_DK ** -0.5)),
        (lambda: wdq_ref[...], diff_norm_rope(dq_ref, dqn_ref, LOG2E * DIFF_DK ** -0.5)),
        (lambda: wmq_ref[...], mem_norm),
        (lambda: wdk_ref[...], diff_norm_rope(dk_ref, dkn_ref, 1.0)),
        (lambda: wdv_ref[...], store(dv_ref)),
    )
    y_next = _dot(h, stages[0][0]())
    for i, (_, epilogue) in enumerate(stages):
        y = y_next
        if i + 1 < len(stages):
            y_next = _dot(h, stages[i + 1][0]())
        epilogue(y)


def _proj(x1, nw, w_rqk, w_in, rcos, rsin, dtab, dqn, dkn, mqn, cast, *, tm=512):
    t, d = x1.shape
    cast_specs, cast_shapes = _cast_specs(cast, t // tm)
    widths = (RET_QK_W, RET_QK_W, RET_V_W, DIFF_QK_W, DIFF_QK_W, DIFF_V_W, MEM_Q_W)

    def row(w):
        return pl.BlockSpec((tm, w), lambda i: (i, 0))

    outs = pl.pallas_call(
        functools.partial(_proj_body, n_cast=len(cast)),
        out_shape=(*(jax.ShapeDtypeStruct((t, w), BF16) for w in widths), *cast_shapes),
        grid=(t // tm,),
        in_specs=[row(d), _resident(nw.shape), _resident(w_rqk.shape),
                  _resident_cols(d, RET_V_W, 1),
                  _resident_cols(d, DIFF_QK_W, 6), _resident_cols(d, DIFF_QK_W, 7),
                  _resident_cols(d, DIFF_V_W, 8), _resident_cols(d, MEM_Q_W, 9),
                  row(LANES), row(LANES), pl.BlockSpec((LANES, tm), lambda i: (0, i)),
                  _resident(dqn.shape), _resident(dkn.shape), _resident(mqn.shape), *cast_specs],
        out_specs=(*(row(w) for w in widths), *cast_specs),
        compiler_params=_params(("arbitrary",)),
        name="proj",
    )(x1, nw, w_rqk, w_in, w_in, w_in, w_in, w_in, rcos, rsin, dtab, dqn, dkn, mqn, *cast)
    return outs[:len(widths)], outs[len(widths):]


def _memkv_body(m_ref, nw_ref, w_ref, kn_ref, pos_ref, rinv_ref, *rest, n_cast):
    cast_in, (mk_ref, mv_ref, rcos_ref, rsin_ref), cast_out = (
        rest[:n_cast], rest[n_cast:n_cast + 4], rest[n_cast + 4:])
    _cast_chunks(cast_in, cast_out)
    ang = pos_ref[...] * rinv_ref[...]
    rcos_ref[...] = jnp.cos(ang)
    rsin_ref[...] = jnp.sin(ang)
    hm = _rms(m_ref[...], nw_ref[...]).astype(BF16)
    k = _dot(hm, w_ref[:, 0:MEM_Q_W].astype(BF16))
    gain = kn_ref[...] * (MEM_DH ** -0.5)
    for hd in range(MEM_HEADS):
        kh = k[:, hd * MEM_DH:(hd + 1) * MEM_DH]
        mk_ref[:, hd * MEM_DH:(hd + 1) * MEM_DH] = (_rms(kh) * gain).astype(BF16)
    mv_ref[...] = _dot(hm, w_ref[:, MEM_Q_W:2 * MEM_Q_W].astype(BF16)).astype(BF16)


def _memkv(mem2d, nw, w_kv, kn, pos, rinv, cast, *, tm=256):
    t, d = mem2d.shape
    steps = t // tm
    tok = pos.shape[0] // steps
    row = pl.BlockSpec((tm, d), lambda i: (i, 0))
    out = pl.BlockSpec((tm, MEM_Q_W), lambda i: (i, 0))
    tab = pl.BlockSpec((tok, LANES), lambda i: (i, 0))
    cast_specs, cast_shapes = _cast_specs(cast, steps)
    outs = pl.pallas_call(
        functools.partial(_memkv_body, n_cast=len(cast)),
        out_shape=(jax.ShapeDtypeStruct((t, MEM_Q_W), BF16),) * 2
        + (jax.ShapeDtypeStruct((pos.shape[0], LANES), F32),) * 2 + tuple(cast_shapes),
        grid=(steps,),
        in_specs=[row, _resident(nw.shape), _resident(w_kv.shape), _resident(kn.shape),
                  pl.BlockSpec((tok, 1), lambda i: (i, 0)), _resident(rinv.shape), *cast_specs],
        out_specs=(out, out, tab, tab, *cast_specs),
        compiler_params=_params(("arbitrary",)),
        name="memkv",
    )(mem2d, nw, w_kv, kn, pos, rinv, *cast)
    return outs[0], outs[1], outs[2], outs[3], outs[4:]


def _ret_body(q_ref, k_ref, v_ref, dmat_ref, qdec_ref, kdec_ref, cdec_ref, o_ref, state_ref, *, seq):
    dmat = dmat_ref[0]
    qdec = qdec_ref[0]
    kdec = kdec_ref[0]
    cdec = cdec_ref[0]

    def local(n):
        rows = slice(n * RET_BLOCK, (n + 1) * RET_BLOCK)
        q = q_ref[rows, :]
        k = k_ref[rows, :]
        v = v_ref[rows, :]
        s = _dot_nt(q, k) * dmat
        o = _dot(s.astype(BF16), v)
        upd = _dot_tn((k.astype(F32) * kdec).astype(BF16), v)
        qd = (q.astype(F32) * qdec).astype(BF16) if n else None
        return o, upd, qd

    nblk = seq // RET_BLOCK
    nxt = local(0)
    for n in range(nblk):
        rows = slice(n * RET_BLOCK, (n + 1) * RET_BLOCK)
        o, upd, qd = nxt
        if n + 1 < nblk:
            nxt = local(n + 1)
        if n == 0:
            state_ref[...] = upd
        else:
            st = state_ref[...]
            o = o + _dot(qd, st.astype(BF16))
            state_ref[...] = st * cdec + upd
        o_ref[rows, :] = _rms(o).astype(BF16)


def _retention(rq, rk, rv, dmat, qdec, kdec, cdec, *, batch, seq):
    qk = pl.BlockSpec((seq, RET_DK), lambda b, h: (b, h))
    vv = pl.BlockSpec((seq, RET_DV), lambda b, h: (b, h))

    def per_head(shape):
        return pl.BlockSpec((1,) + shape[1:], lambda b, h: (h, 0, 0))

    return pl.pallas_call(
        functools.partial(_ret_body, seq=seq),
        out_shape=jax.ShapeDtypeStruct(rv.shape, BF16),
        grid=(batch, RET_HEADS),
        in_specs=[qk, qk, vv, per_head(dmat.shape), per_head(qdec.shape), per_head(kdec.shape),
                  per_head(cdec.shape)],
        out_specs=vv,
        scratch_shapes=[pltpu.VMEM((RET_DK, RET_DV), F32)],
        compiler_params=_params(("parallel", "parallel")),
        name="retention",
    )(rq, rk, rv, dmat, qdec, kdec, cdec)


def _diff_body(q_ref, k_ref, v_ref, lamv_ref, subln_ref, o_ref, vx_ref, *, seq):
    lv = lamv_ref[...]
    lam = (jnp.exp(jnp.sum(lv[0:1] * lv[1:2], axis=-1, keepdims=True))
           - jnp.exp(jnp.sum(lv[2:3] * lv[3:4], axis=-1, keepdims=True)) + LAM_INIT)
    gain = subln_ref[...] * (1.0 - LAM_INIT)
    tq = DIFF_QBLOCK
    low = lax.broadcasted_iota(jnp.int32, (1, LANES), 1) < DIFF_DK
    rchunk = (lax.broadcasted_iota(jnp.int32, (2 * tq, tq), 0) % tq) // CHUNK
    cchunk = lax.broadcasted_iota(jnp.int32, (2 * tq, tq), 1) // CHUNK
    visible = cchunk <= rchunk
    zero = jnp.zeros((), BF16)

    nq = seq // tq
    for hd in range(DIFF_HEADS_PER_STEP):
        vx_ref[hd, :, 0:DIFF_DV] = v_ref[:, hd * DIFF_DV:(hd + 1) * DIFF_DV]
        vx_ref[hd, :, DIFF_DV:2 * DIFF_DV] = jnp.ones((seq, DIFF_DV), BF16)

    def scores(item):
        hd, qi = item
        cols = slice(hd * LANES, (hd + 1) * LANES)
        q = q_ref[qi * tq:(qi + 1) * tq, cols]
        q2 = jnp.concatenate([jnp.where(low, q, zero), jnp.where(low, zero, q)], axis=0)
        sd = jnp.where(visible, _dot_nt(q2, k_ref[qi * tq:(qi + 1) * tq, cols]), -jnp.inf)
        sp = _dot_nt(q2, k_ref[0:qi * tq, cols]) if qi else None
        return sd, sp

    items = [(hd, qi) for hd in range(DIFF_HEADS_PER_STEP) for qi in reversed(range(nq))]
    ahead = [scores(item) for item in items[:DIFF_LOOKAHEAD]]
    for step, (hd, qi) in enumerate(items):
        rows = slice(qi * tq, (qi + 1) * tq)
        past = qi * tq
        sd, sp = ahead.pop(0)
        if step + DIFF_LOOKAHEAD < len(items):
            ahead.append(scores(items[step + DIFF_LOOKAHEAD]))
        m = jnp.max(sd, axis=-1, keepdims=True)
        if past:
            m = jnp.maximum(m, jnp.max(sp, axis=-1, keepdims=True))
        ol = _dot(jnp.exp2(sd - m).astype(BF16), vx_ref[hd, rows, :])
        if past:
            ol = ol + _dot(jnp.exp2(sp - m).astype(BF16), vx_ref[hd, 0:past, :])
        o0 = ol[0:tq, 0:DIFF_DV] / ol[0:tq, DIFF_DV:2 * DIFF_DV]
        o1 = ol[tq:2 * tq, 0:DIFF_DV] / ol[tq:2 * tq, DIFF_DV:2 * DIFF_DV]
        o_ref[rows, hd * DIFF_DV:(hd + 1) * DIFF_DV] = (_rms(o0 - lam * o1) * gain).astype(BF16)


def _diff_attention(dq, dk, dv, lamv, subln, *, batch, seq):
    blk = pl.BlockSpec((seq, DIFF_HEADS_PER_STEP * LANES), lambda b, h: (b, h))
    return pl.pallas_call(
        functools.partial(_diff_body, seq=seq),
        out_shape=jax.ShapeDtypeStruct(dv.shape, BF16),
        grid=(batch, DIFF_HEADS // DIFF_HEADS_PER_STEP),
        in_specs=[blk, blk, blk, _resident(lamv.shape), _resident(subln.shape)],
        out_specs=blk,
        scratch_shapes=[pltpu.VMEM((DIFF_HEADS_PER_STEP, seq, 2 * DIFF_DV), BF16)],
        compiler_params=_params(("parallel", "parallel")),
        name="diff_attention",
    )(dq, dk, dv, lamv, subln)


def _merge_body(x_ref, ro_ref, do_ref, mq_ref, mk_ref, mv_ref, nw_ref, wrg_ref, wg0_ref, wg1_ref, wg2_ref, bg_ref,
                wro_ref, wdo_ref, wmo_ref, wout_ref, o_ref):
    for r in range(0, x_ref.shape[0], MERGE_SUBROWS):
        rows = slice(r, r + MERGE_SUBROWS)
        heads = [slice(hd * MEM_DH, (hd + 1) * MEM_DH) for hd in range(MEM_HEADS)]
        scores = [_dot_nt(mq_ref[rows, cols], mk_ref[:, cols]) for cols in heads]
        x = x_ref[rows, :]
        h = _rms(x, nw_ref[...]).astype(BF16)
        rg = _dot(h, wrg_ref[...])
        probs = []
        for s in scores:
            p = jnp.exp(s - jnp.max(s, axis=-1, keepdims=True))
            probs.append((p * (1.0 / jnp.sum(p, axis=-1, keepdims=True))).astype(BF16))
        g = [jax.nn.sigmoid(_dot(h, w_ref[...]) + bg_ref[:, i * D_MODEL:(i + 1) * D_MODEL])
             for i, w_ref in enumerate((wg0_ref, wg1_ref, wg2_ref))]
        mem_heads = [_dot(p, mv_ref[:, cols]).astype(BF16) for p, cols in zip(probs, heads)]
        ro = (ro_ref[rows, :].astype(F32) * (rg * jax.nn.sigmoid(rg))).astype(BF16)
        ret_out = _dot(ro, wro_ref[...])
        mem_out = None
        for oh, cols in zip(mem_heads, heads):
            part = _dot(oh, wmo_ref[cols, :])
            mem_out = part if mem_out is None else mem_out + part
        diff_out = _dot(do_ref[rows, :], wdo_ref[...])
        merged = g[0] * ret_out + g[1] * diff_out + g[2] * mem_out
        o_ref[rows, :] = x + _dot(merged.astype(BF16), wout_ref[...])


def _merge(x1, ro, do, mq, mk, mv, nw, w_in, bg, wro, wdo, wmo, wout, *, seq, mem_len, tm=512):
    t, d = x1.shape
    per_b = seq // tm

    def row(w):
        return pl.BlockSpec((tm, w), lambda i: (i, 0))

    memb = pl.BlockSpec((mem_len, MEM_Q_W), lambda i: (i // per_b, 0))
    return pl.pallas_call(
        _merge_body,
        out_shape=jax.ShapeDtypeStruct((t, d), F32),
        grid=(t // tm,),
        in_specs=[row(d), row(RET_V_W), row(DIFF_V_W), row(MEM_Q_W), memb, memb,
                  _resident(nw.shape), _resident_cols(d, RET_V_W, 2),
                  _resident_cols(d, D_MODEL, QKV_W // D_MODEL),
                  _resident_cols(d, D_MODEL, QKV_W // D_MODEL + 1),
                  _resident_cols(d, D_MODEL, QKV_W // D_MODEL + 2), _resident(bg.shape), _resident(wro.shape),
                  _resident(wdo.shape), _resident(wmo.shape), _resident(wout.shape)],
        out_specs=row(d),
        compiler_params=_params(("parallel",)),
        name="merge",
    )(x1, ro, do, mq, mk, mv, nw, w_in, w_in, w_in, w_in, bg, wro, wdo, wmo, wout)


def _deinterleave_matrix():
    j = jnp.arange(RET_DK)
    src = jnp.where(j < RET_DK // 2, 2 * j, 2 * (j - RET_DK // 2) + 1)
    return (jnp.arange(RET_DK)[:, None] == src[None, :]).astype(BF16)


def _rotation_tables(positions):
    pos = positions.astype(F32).reshape(-1, 1)
    ret_inv = (1.0 / (RET_THETA_BASE ** jnp.linspace(0.0, 1.0, RET_DK // 2, dtype=F32))).reshape(1, -1)
    rope_inv = 1.0 / (ROPE_THETA ** (jnp.arange(0, ROT_DIM, 2, dtype=F32) / ROT_DIM))
    d_ang = rope_inv[:, None] * pos.reshape(1, -1)
    dtab = jnp.concatenate([jnp.cos(d_ang), jnp.sin(d_ang), jnp.zeros((LANES - ROT_DIM, pos.shape[0]), F32)], axis=0)
    return pos, ret_inv, dtab


def _decay_tables():
    log_g = jnp.log(1.0 - 2.0 ** (-5.0 - jnp.arange(RET_HEADS, dtype=F32)))
    idx = jnp.arange(RET_BLOCK, dtype=F32)
    dist = jnp.abs(idx[:, None] - idx[None, :])
    chunk = jnp.arange(RET_BLOCK) // CHUNK
    visible = chunk[None, :] <= chunk[:, None]
    dmat = jnp.where(visible[None], jnp.exp(log_g[:, None, None] * dist[None]), 0.0)
    qdec = jnp.exp(log_g[:, None] * (idx[None, :] + 1.0))[..., None]
    kdec = jnp.exp(log_g[:, None] * (RET_BLOCK - 1.0 - idx[None, :]))[..., None]
    cdec = jnp.broadcast_to(jnp.exp(log_g * RET_BLOCK)[:, None, None], (RET_HEADS, 1, RET_DV))
    return dmat, qdec, kdec, cdec


def kernel(x, mem, positions, ffn1_norm, ffn1_w_gate, ffn1_w_up, ffn1_w_down, mix_norm, w_in, b_gate, ret_w_o, diff_q_norm, diff_k_norm, diff_lambda_q1, diff_lambda_k1, diff_lambda_q2, diff_lambda_k2, diff_subln, diff_w_o, mem_norm, mem_w_kv, mem_q_norm, mem_k_norm, mem_w_o, w_out, ffn2_norm, ffn2_w_gate, ffn2_w_up, ffn2_w_down, final_norm):
    batch, seq, d = x.shape
    mem_len = mem.shape[1]
    assert d == D_MODEL and seq % RET_BLOCK == 0 and seq % DIFF_QBLOCK == 0
    assert w_in.shape[0] == 1, "single-layer trunk"
    vec = lambda a: a.reshape(1, -1)

    pos, ret_inv, dtab = _rotation_tables(positions)
    dmat, qdec, kdec, cdec = _decay_tables()
    perm = _deinterleave_matrix()
    dqn = jnp.tile(vec(diff_q_norm[0]), (1, 2))
    dkn = jnp.tile(vec(diff_k_norm[0]), (1, 2))
    lamv = jnp.stack([diff_lambda_q1[0], diff_lambda_k1[0], diff_lambda_q2[0], diff_lambda_k2[0]]).astype(F32)

    mk, mv, rcos, rsin, (w1g, w1u, w1d) = _memkv(
        mem.reshape(batch * mem_len, d), vec(mem_norm[0]), mem_w_kv[0], vec(mem_k_norm[0]), pos, ret_inv,
        (ffn1_w_gate[0], ffn1_w_up[0], ffn1_w_down[0]))
    x1, w_rqk, (wi, w_ro, w_do, w_mo, w_o) = _ffn_first(
        x.reshape(batch * seq, d), vec(ffn1_norm[0]), w1g, w1u, w1d, perm,
        (w_in[0], ret_w_o[0], diff_w_o[0], mem_w_o[0], w_out[0]))
    (rq, rk, rv, dq, dk, dv, mq), (w2g, w2u, w2d) = _proj(
        x1, vec(mix_norm[0]), w_rqk, wi, rcos, rsin, dtab, dqn, dkn, vec(mem_q_norm[0]),
        (ffn2_w_gate[0], ffn2_w_up[0], ffn2_w_down[0]))
    ro = _retention(rq, rk, rv, dmat, qdec, kdec, cdec, batch=batch, seq=seq)
    do = _diff_attention(dq, dk, dv, lamv, vec(diff_subln[0]), batch=batch, seq=seq)
    x2 = _merge(x1, ro, do, mq, mk, mv, vec(mix_norm[0]), wi, vec(b_gate[0]), w_ro, w_do, w_mo, w_o,
                seq=seq, mem_len=mem_len)
    x3 = _ffn_last(x2, vec(ffn2_norm[0]), w2g, w2u, w2d, vec(final_norm[0]))
    return x3.reshape(batch, seq, d)
```

```python
import functools
import math

import jax
import jax.numpy as jnp
from jax import lax
from jax.experimental import pallas as pl
from jax.experimental.pallas import tpu as pltpu

D_MODEL = 1024
CHUNK = 64
EPS = 1e-6
RET_HEADS = 4
RET_DK = 256
RET_DV = 512
RET_THETA_BASE = 10000.0
DIFF_HEADS = 8
DIFF_DK = 64
DIFF_DV = 128
ROPE_THETA = 500000.0
ROT_DIM = 16
MEM_HEADS = 4
MEM_DH = 256
D_FF = 2816
LAM_INIT = 0.8 - 0.6 * math.exp(-0.3 * 0)
LOG2E = math.log2(math.e)

RET_QK_W = RET_HEADS * RET_DK
RET_V_W = RET_HEADS * RET_DV
DIFF_QK_W = DIFF_HEADS * 2 * DIFF_DK
DIFF_V_W = DIFF_HEADS * DIFF_DV
MEM_Q_W = MEM_HEADS * MEM_DH
QKV_W = 2 * RET_QK_W + 2 * RET_V_W + 2 * DIFF_QK_W + DIFF_V_W + MEM_Q_W
GATE_W = 3 * D_MODEL

LANES = 128
BF16_SUBLANES = 16
FFN_SUBROWS = 256
PROJ_SUBROWS = 256
MERGE_SUBROWS = 256
RET_BLOCK = 256
RET_HEADS_PER_STEP = 1
DIFF_QBLOCK = 256
DIFF_HEADS_PER_STEP = 2
DIFF_LOOKAHEAD = 1
VMEM_LIMIT = 56 * 1024 * 1024

BF16 = jnp.bfloat16
F32 = jnp.float32


def _dot(a, b):
    return jnp.dot(a, b, preferred_element_type=F32)


def _dot_nt(a, b):
    return lax.dot_general(a, b, (((1,), (1,)), ((), ())), preferred_element_type=F32)


def _dot_tn(a, b):
    return lax.dot_general(a, b, (((0,), (0,)), ((), ())), preferred_element_type=F32)


def _rms(x, w=None):
    y = x * lax.rsqrt(jnp.mean(x * x, axis=-1, keepdims=True) + EPS)
    if w is not None:
        y = y * w
    return y


def _resident(shape):
    nd = len(shape)
    return pl.BlockSpec(shape, lambda *_: (0,) * nd, pipeline_mode=pl.Buffered(1))


def _resident_cols(rows, width, index):
    return pl.BlockSpec((rows, width), lambda *_: (0, index), pipeline_mode=pl.Buffered(1))


def _params(sem):
    return pltpu.CompilerParams(dimension_semantics=sem, vmem_limit_bytes=VMEM_LIMIT)


def _swiglu_half_step(x, nw_ref, wg_ref, wu_ref, wd_ref):
    h = _rms(x, nw_ref[...]).astype(BF16)
    g = _dot(h, wg_ref[...])
    u = _dot(h, wu_ref[...])
    a = (g * jax.nn.sigmoid(g) * u).astype(BF16)
    return x + 0.5 * _dot(a, wd_ref[...])


def _cast_chunks(src_refs, dst_refs):
    for src, dst in zip(src_refs, dst_refs):
        dst[...] = src[...].astype(BF16)


def _cast_specs(mats, steps):
    specs, shapes = [], []
    for m in mats:
        rows, cols = m.shape
        chunk = next(c for c in range(BF16_SUBLANES, rows + 1, BF16_SUBLANES)
                     if rows % c == 0 and rows // c <= steps)
        last = rows // chunk - 1
        specs.append(pl.BlockSpec((chunk, cols), lambda i, last=last: (jnp.minimum(i, last), 0)))
        shapes.append(jax.ShapeDtypeStruct((rows, cols), BF16))
    return specs, shapes


def _ffn_first_body(x_ref, nw_ref, wg_ref, wu_ref, wd_ref, perm_ref, *rest, n_cast):
    cast_in, (o_ref, wrqk_ref), cast_out = rest[:n_cast], rest[n_cast:n_cast + 2], rest[n_cast + 2:]
    _cast_chunks(cast_in, cast_out)
    perm = perm_ref[...]
    for hb in range(2 * RET_HEADS):
        cols = slice(hb * RET_DK, (hb + 1) * RET_DK)
        wrqk_ref[:, cols] = _dot(cast_in[0][:, cols].astype(BF16), perm).astype(BF16)
    for r in range(0, x_ref.shape[0], FFN_SUBROWS):
        rows = slice(r, r + FFN_SUBROWS)
        o_ref[rows, :] = _swiglu_half_step(x_ref[rows, :], nw_ref, wg_ref, wu_ref, wd_ref)


def _ffn_first(x, nw, wg, wu, wd, perm, cast, *, tm=1024):
    t, d = x.shape
    steps = t // tm
    row = pl.BlockSpec((tm, d), lambda i: (i, 0))
    cast_specs, cast_shapes = _cast_specs(cast, steps)
    w_in_chunk = cast_specs[0].block_shape[0]
    assert cast[0].shape[0] == w_in_chunk * steps, "every step must see a fresh row chunk of w_in"
    wrqk_spec = pl.BlockSpec((w_in_chunk, 2 * RET_QK_W), lambda i: (i, 0))
    outs = pl.pallas_call(
        functools.partial(_ffn_first_body, n_cast=len(cast)),
        out_shape=(jax.ShapeDtypeStruct((t, d), F32),
                   jax.ShapeDtypeStruct((cast[0].shape[0], 2 * RET_QK_W), BF16), *cast_shapes),
        grid=(steps,),
        in_specs=[row, _resident(nw.shape), _resident(wg.shape), _resident(wu.shape), _resident(wd.shape),
                  _resident(perm.shape), *cast_specs],
        out_specs=(row, wrqk_spec, *cast_specs),
        compiler_params=_params(("arbitrary",)),
        name="ffn_first",
    )(x, nw, wg, wu, wd, perm, *cast)
    return outs[0], outs[1], outs[2:]


def _ffn_last_body(x_ref, nw_ref, wg_ref, wu_ref, wd_ref, fw_ref, o_ref):
    for r in range(0, x_ref.shape[0], FFN_SUBROWS):
        rows = slice(r, r + FFN_SUBROWS)
        o_ref[rows, :] = _rms(_swiglu_half_step(x_ref[rows, :], nw_ref, wg_ref, wu_ref, wd_ref), fw_ref[...])


def _ffn_last(x, nw, wg, wu, wd, fw, *, tm=1024):
    t, d = x.shape
    row = pl.BlockSpec((tm, d), lambda i: (i, 0))
    return pl.pallas_call(
        _ffn_last_body,
        out_shape=jax.ShapeDtypeStruct((t, d), F32),
        grid=(t // tm,),
        in_specs=[row, _resident(nw.shape), _resident(wg.shape), _resident(wu.shape),
                  _resident(wd.shape), _resident(fw.shape)],
        out_specs=row,
        compiler_params=_params(("parallel",)),
        name="ffn_last",
    )(x, nw, wg, wu, wd, fw)


def _proj_body(x_ref, nw_ref, wrqk_ref, wrv_ref, wdq_ref, wdk_ref, wdv_ref, wmq_ref,
               rcos_ref, rsin_ref, dtab_ref, dqn_ref, dkn_ref, mqn_ref, *rest, n_cast):
    cast_in, cast_out = rest[:n_cast], rest[n_cast + 7:]
    rq_ref, rk_ref, rv_ref, dq_ref, dk_ref, dv_ref, mq_ref = rest[n_cast:n_cast + 7]
    lane = lax.broadcasted_iota(jnp.int32, (1, LANES), 1)
    half = ROT_DIM // 2
    low = lane < DIFF_DK
    first = (lane % DIFF_DK) < half
    for r in range(0, x_ref.shape[0], PROJ_SUBROWS):
        _proj_rows(slice(r, r + PROJ_SUBROWS), lane, half, low, first, x_ref, nw_ref, wrqk_ref, wrv_ref, wdq_ref,
                   wdk_ref, wdv_ref, wmq_ref, rcos_ref, rsin_ref, dtab_ref, dqn_ref, dkn_ref, mqn_ref,
                   rq_ref, rk_ref, rv_ref, dq_ref, dk_ref, dv_ref, mq_ref)
    _cast_chunks(cast_in, cast_out)


def _proj_rows(rows, lane, half, low, first, x_ref, nw_ref, wrqk_ref, wrv_ref, wdq_ref, wdk_ref, wdv_ref, wmq_ref,
               rcos_ref, rsin_ref, dtab_ref, dqn_ref, dkn_ref, mqn_ref,
               rq_ref, rk_ref, rv_ref, dq_ref, dk_ref, dv_ref, mq_ref):
    h = _rms(x_ref[rows, :], nw_ref[...]).astype(BF16)
    rcos = rcos_ref[rows, :]
    rsin = rsin_ref[rows, :]

    dtab = dtab_ref[:, rows].T
    c = jnp.where(lane < half, dtab, 0.0)
    c = c + pltpu.roll(c, half, 1)
    c = c + pltpu.roll(c, DIFF_DK, 1)
    dcos = jnp.where((lane % DIFF_DK) < ROT_DIM, c, 1.0)
    sn = jnp.where((lane >= half) & (lane < ROT_DIM), dtab, 0.0)
    sn = sn - pltpu.roll(sn, LANES - half, 1)
    dsin = sn + pltpu.roll(sn, DIFF_DK, 1)

    def ret_rotate(out_ref, scale):
        def epilogue(y):
            for hd in range(RET_HEADS):
                ye = y[:, hd * RET_DK:hd * RET_DK + LANES]
                yo = y[:, hd * RET_DK + LANES:(hd + 1) * RET_DK]
                out_ref[rows, hd * RET_DK:hd * RET_DK + LANES] = ((ye * rcos - yo * rsin) * scale).astype(BF16)
                out_ref[rows, hd * RET_DK + LANES:(hd + 1) * RET_DK] = ((yo * rcos + ye * rsin) * scale).astype(BF16)
        return epilogue

    def store(out_ref):
        def epilogue(y):
            out_ref[rows, :] = y.astype(BF16)
        return epilogue

    def diff_norm_rope(out_ref, n_ref, scale):
        def epilogue(y):
            gain = n_ref[...]
            for hd in range(DIFF_HEADS):
                yh = y[:, hd * LANES:(hd + 1) * LANES]
                sq = yh * yh
                ss_lo = jnp.sum(jnp.where(low, sq, 0.0), axis=-1, keepdims=True)
                ss_hi = jnp.sum(jnp.where(low, 0.0, sq), axis=-1, keepdims=True)
                r = jnp.where(low, lax.rsqrt(ss_lo * (1.0 / DIFF_DK) + EPS),
                              lax.rsqrt(ss_hi * (1.0 / DIFF_DK) + EPS))
                yn = yh * r * gain
                partner = jnp.where(first, pltpu.roll(yn, LANES - half, 1), pltpu.roll(yn, half, 1))
                out_ref[rows, hd * LANES:(hd + 1) * LANES] = ((yn * dcos + partner * dsin) * scale).astype(BF16)
        return epilogue

    def mem_norm(y):
        gain = mqn_ref[...]
        for hd in range(MEM_HEADS):
            yh = y[:, hd * MEM_DH:(hd + 1) * MEM_DH]
            mq_ref[rows, hd * MEM_DH:(hd + 1) * MEM_DH] = (_rms(yh) * gain).astype(BF16)

    stages = (
        (lambda: wrqk_ref[:, 0:RET_QK_W], ret_rotate(rq_ref, 1.0)),
        (lambda: wrv_ref[...], store(rv_ref)),
        (lambda: wrqk_ref[:, RET_QK_W:2 * RET_QK_W], ret_rotate(rk_ref, RET_DK ** -0.5)),
        (lambda: wdq_ref[...], diff_norm_rope(dq_ref, dqn_ref, LOG2E * DIFF_DK ** -0.5)),
        (lambda: wmq_ref[...], mem_norm),
        (lambda: wdk_ref[...], diff_norm_rope(dk_ref, dkn_ref, 1.0)),
        (lambda: wdv_ref[...], store(dv_ref)),
    )
    y_next = _dot(h, stages[0][0]())
    for i, (_, epilogue) in enumerate(stages):
        y = y_next
        if i + 1 < len(stages):
            y_next = _dot(h, stages[i + 1][0]())
        epilogue(y)


def _proj(x1, nw, w_rqk, w_in, rcos, rsin, dtab, dqn, dkn, mqn, cast, *, tm=512):
    t, d = x1.shape
    cast_specs, cast_shapes = _cast_specs(cast, t // tm)
    widths = (RET_QK_W, RET_QK_W, RET_V_W, DIFF_QK_W, DIFF_QK_W, DIFF_V_W, MEM_Q_W)

    def row(w):
        return pl.BlockSpec((tm, w), lambda i: (i, 0))

    outs = pl.pallas_call(
        functools.partial(_proj_body, n_cast=len(cast)),
        out_shape=(*(jax.ShapeDtypeStruct((t, w), BF16) for w in widths), *cast_shapes),
        grid=(t // tm,),
        in_specs=[row(d), _resident(nw.shape), _resident(w_rqk.shape),
                  _resident_cols(d, RET_V_W, 1),
                  _resident_cols(d, DIFF_QK_W, 6), _resident_cols(d, DIFF_QK_W, 7),
                  _resident_cols(d, DIFF_V_W, 8), _resident_cols(d, MEM_Q_W, 9),
                  row(LANES), row(LANES), pl.BlockSpec((LANES, tm), lambda i: (0, i)),
                  _resident(dqn.shape), _resident(dkn.shape), _resident(mqn.shape), *cast_specs],
        out_specs=(*(row(w) for w in widths), *cast_specs),
        compiler_params=_params(("arbitrary",)),
        name="proj",
    )(x1, nw, w_rqk, w_in, w_in, w_in, w_in, w_in, rcos, rsin, dtab, dqn, dkn, mqn, *cast)
    return outs[:len(widths)], outs[len(widths):]


def _memkv_body(m_ref, nw_ref, w_ref, kn_ref, pos_ref, rinv_ref, *rest, n_cast):
    cast_in, (mk_ref, mv_ref, rcos_ref, rsin_ref), cast_out = (
        rest[:n_cast], rest[n_cast:n_cast + 4], rest[n_cast + 4:])
    _cast_chunks(cast_in, cast_out)
    ang = pos_ref[...] * rinv_ref[...]
    rcos_ref[...] = jnp.cos(ang)
    rsin_ref[...] = jnp.sin(ang)
    hm = _rms(m_ref[...], nw_ref[...]).astype(BF16)
    k = _dot(hm, w_ref[:, 0:MEM_Q_W].astype(BF16))
    gain = kn_ref[...] * (MEM_DH ** -0.5)
    for hd in range(MEM_HEADS):
        kh = k[:, hd * MEM_DH:(hd + 1) * MEM_DH]
        mk_ref[:, hd * MEM_DH:(hd + 1) * MEM_DH] = (_rms(kh) * gain).astype(BF16)
    mv_ref[...] = _dot(hm, w_ref[:, MEM_Q_W:2 * MEM_Q_W].astype(BF16)).astype(BF16)


def _memkv(mem2d, nw, w_kv, kn, pos, rinv, cast, *, tm=256):
    t, d = mem2d.shape
    steps = t // tm
    tok = pos.shape[0] // steps
    row = pl.BlockSpec((tm, d), lambda i: (i, 0))
    out = pl.BlockSpec((tm, MEM_Q_W), lambda i: (i, 0))
    tab = pl.BlockSpec((tok, LANES), lambda i: (i, 0))
    cast_specs, cast_shapes = _cast_specs(cast, steps)
    outs = pl.pallas_call(
        functools.partial(_memkv_body, n_cast=len(cast)),
        out_shape=(jax.ShapeDtypeStruct((t, MEM_Q_W), BF16),) * 2
        + (jax.ShapeDtypeStruct((pos.shape[0], LANES), F32),) * 2 + tuple(cast_shapes),
        grid=(steps,),
        in_specs=[row, _resident(nw.shape), _resident(w_kv.shape), _resident(kn.shape),
                  pl.BlockSpec((tok, 1), lambda i: (i, 0)), _resident(rinv.shape), *cast_specs],
        out_specs=(out, out, tab, tab, *cast_specs),
        compiler_params=_params(("arbitrary",)),
        name="memkv",
    )(mem2d, nw, w_kv, kn, pos, rinv, *cast)
    return outs[0], outs[1], outs[2], outs[3], outs[4:]


def _ret_body(q_ref, k_ref, v_ref, dmat_ref, qdec_ref, kdec_ref, cdec_ref, o_ref, state_ref, *, seq):
    nblk = seq // RET_BLOCK

    def local(item):
        n, hd = item
        rows = slice(n * RET_BLOCK, (n + 1) * RET_BLOCK)
        q = q_ref[rows, hd * RET_DK:(hd + 1) * RET_DK]
        k = k_ref[rows, hd * RET_DK:(hd + 1) * RET_DK]
        v = v_ref[rows, hd * RET_DV:(hd + 1) * RET_DV]
        s = _dot_nt(q, k) * dmat_ref[hd]
        o = _dot(s.astype(BF16), v)
        upd = _dot_tn((k.astype(F32) * kdec_ref[hd]).astype(BF16), v)
        qd = (q.astype(F32) * qdec_ref[hd]).astype(BF16) if n else None
        return o, upd, qd

    items = [(n, hd) for n in range(nblk) for hd in range(RET_HEADS_PER_STEP)]
    nxt = local(items[0])
    for step, (n, hd) in enumerate(items):
        rows = slice(n * RET_BLOCK, (n + 1) * RET_BLOCK)
        o, upd, qd = nxt
        if step + 1 < len(items):
            nxt = local(items[step + 1])
        if n == 0:
            state_ref[hd] = upd
        else:
            st = state_ref[hd]
            o = o + _dot(qd, st.astype(BF16))
            state_ref[hd] = st * cdec_ref[hd] + upd
        o_ref[rows, hd * RET_DV:(hd + 1) * RET_DV] = _rms(o).astype(BF16)


def _diff_body(q_ref, k_ref, v_ref, lamv_ref, subln_ref, o_ref, vx_ref, *, seq):
    lv = lamv_ref[...]
    lam = (jnp.exp(jnp.sum(lv[0:1] * lv[1:2], axis=-1, keepdims=True))
           - jnp.exp(jnp.sum(lv[2:3] * lv[3:4], axis=-1, keepdims=True)) + LAM_INIT)
    gain = subln_ref[...] * (1.0 - LAM_INIT)
    tq = DIFF_QBLOCK
    low = lax.broadcasted_iota(jnp.int32, (1, LANES), 1) < DIFF_DK
    rchunk = (lax.broadcasted_iota(jnp.int32, (2 * tq, tq), 0) % tq) // CHUNK
    cchunk = lax.broadcasted_iota(jnp.int32, (2 * tq, tq), 1) // CHUNK
    visible = cchunk <= rchunk
    zero = jnp.zeros((), BF16)

    nq = seq // tq
    for hd in range(DIFF_HEADS_PER_STEP):
        vx_ref[hd, :, 0:DIFF_DV] = v_ref[:, hd * DIFF_DV:(hd + 1) * DIFF_DV]
        vx_ref[hd, :, DIFF_DV:2 * DIFF_DV] = jnp.ones((seq, DIFF_DV), BF16)

    def scores(item):
        hd, qi = item
        cols = slice(hd * LANES, (hd + 1) * LANES)
        q = q_ref[qi * tq:(qi + 1) * tq, cols]
        q2 = jnp.concatenate([jnp.where(low, q, zero), jnp.where(low, zero, q)], axis=0)
        sd = jnp.where(visible, _dot_nt(q2, k_ref[qi * tq:(qi + 1) * tq, cols]), -jnp.inf)
        sp = _dot_nt(q2, k_ref[0:qi * tq, cols]) if qi else None
        return sd, sp

    items = [(hd, qi) for hd in range(DIFF_HEADS_PER_STEP) for qi in reversed(range(nq))]
    ahead = [scores(item) for item in items[:DIFF_LOOKAHEAD]]
    for step, (hd, qi) in enumerate(items):
        rows = slice(qi * tq, (qi + 1) * tq)
        past = qi * tq
        sd, sp = ahead.pop(0)
        if step + DIFF_LOOKAHEAD < len(items):
            ahead.append(scores(items[step + DIFF_LOOKAHEAD]))
        m = jnp.max(sd, axis=-1, keepdims=True)
        if past:
            m = jnp.maximum(m, jnp.max(sp, axis=-1, keepdims=True))
        ol = _dot(jnp.exp2(sd - m).astype(BF16), vx_ref[hd, rows, :])
        if past:
            ol = ol + _dot(jnp.exp2(sp - m).astype(BF16), vx_ref[hd, 0:past, :])
        o0 = ol[0:tq, 0:DIFF_DV] / ol[0:tq, DIFF_DV:2 * DIFF_DV]
        o1 = ol[tq:2 * tq, 0:DIFF_DV] / ol[tq:2 * tq, DIFF_DV:2 * DIFF_DV]
        o_ref[rows, hd * DIFF_DV:(hd + 1) * DIFF_DV] = (_rms(o0 - lam * o1) * gain).astype(BF16)


def _mixers_body(rq_ref, rk_ref, rv_ref, dmat_ref, qdec_ref, kdec_ref, cdec_ref,
                 dq_ref, dk_ref, dv_ref, lamv_ref, subln_ref, ro_ref, do_ref, state_ref, vx_ref, *, seq):
    _ret_body(rq_ref, rk_ref, rv_ref, dmat_ref, qdec_ref, kdec_ref, cdec_ref, ro_ref, state_ref, seq=seq)
    _diff_body(dq_ref, dk_ref, dv_ref, lamv_ref, subln_ref, do_ref, vx_ref, seq=seq)


def _mixers(rq, rk, rv, dmat, qdec, kdec, cdec, dq, dk, dv, lamv, subln, *, batch, seq):
    steps = RET_HEADS // RET_HEADS_PER_STEP
    assert steps == DIFF_HEADS // DIFF_HEADS_PER_STEP
    qk = pl.BlockSpec((seq, RET_HEADS_PER_STEP * RET_DK), lambda b, h: (b, h))
    vv = pl.BlockSpec((seq, RET_HEADS_PER_STEP * RET_DV), lambda b, h: (b, h))
    blk = pl.BlockSpec((seq, DIFF_HEADS_PER_STEP * LANES), lambda b, h: (b, h))

    def per_head(shape):
        return pl.BlockSpec((RET_HEADS_PER_STEP,) + shape[1:], lambda b, h: (h, 0, 0))

    return pl.pallas_call(
        functools.partial(_mixers_body, seq=seq),
        out_shape=(jax.ShapeDtypeStruct(rv.shape, BF16), jax.ShapeDtypeStruct(dv.shape, BF16)),
        grid=(batch, steps),
        in_specs=[qk, qk, vv, per_head(dmat.shape), per_head(qdec.shape), per_head(kdec.shape),
                  per_head(cdec.shape), blk, blk, blk, _resident(lamv.shape), _resident(subln.shape)],
        out_specs=(vv, blk),
        scratch_shapes=[pltpu.VMEM((RET_HEADS_PER_STEP, RET_DK, RET_DV), F32),
                        pltpu.VMEM((DIFF_HEADS_PER_STEP, seq, 2 * DIFF_DV), BF16)],
        compiler_params=_params(("parallel", "parallel")),
        name="mixers",
    )(rq, rk, rv, dmat, qdec, kdec, cdec, dq, dk, dv, lamv, subln)


def _merge_body(x_ref, ro_ref, do_ref, mq_ref, mk_ref, mv_ref, nw_ref, wrg_ref, wg0_ref, wg1_ref, wg2_ref, bg_ref,
                wro_ref, wdo_ref, wmo_ref, wout_ref, o_ref):
    for r in range(0, x_ref.shape[0], MERGE_SUBROWS):
        rows = slice(r, r + MERGE_SUBROWS)
        heads = [slice(hd * MEM_DH, (hd + 1) * MEM_DH) for hd in range(MEM_HEADS)]
        scores = [_dot_nt(mq_ref[rows, cols], mk_ref[:, cols]) for cols in heads]
        x = x_ref[rows, :]
        h = _rms(x, nw_ref[...]).astype(BF16)
        rg = _dot(h, wrg_ref[...])
        probs = []
        for s in scores:
            p = jnp.exp(s - jnp.max(s, axis=-1, keepdims=True))
            probs.append((p * (1.0 / jnp.sum(p, axis=-1, keepdims=True))).astype(BF16))
        g = [jax.nn.sigmoid(_dot(h, w_ref[...]) + bg_ref[:, i * D_MODEL:(i + 1) * D_MODEL])
             for i, w_ref in enumerate((wg0_ref, wg1_ref, wg2_ref))]
        mem_heads = [_dot(p, mv_ref[:, cols]).astype(BF16) for p, cols in zip(probs, heads)]
        ro = (ro_ref[rows, :].astype(F32) * (rg * jax.nn.sigmoid(rg))).astype(BF16)
        ret_out = _dot(ro, wro_ref[...])
        mem_out = None
        for oh, cols in zip(mem_heads, heads):
            part = _dot(oh, wmo_ref[cols, :])
            mem_out = part if mem_out is None else mem_out + part
        diff_out = _dot(do_ref[rows, :], wdo_ref[...])
        merged = g[0] * ret_out + g[1] * diff_out + g[2] * mem_out
        o_ref[rows, :] = x + _dot(merged.astype(BF16), wout_ref[...])


def _merge(x1, ro, do, mq, mk, mv, nw, w_in, bg, wro, wdo, wmo, wout, *, seq, mem_len, tm=512):
    t, d = x1.shape
    per_b = seq // tm

    def row(w):
        return pl.BlockSpec((tm, w), lambda i: (i, 0))

    memb = pl.BlockSpec((mem_len, MEM_Q_W), lambda i: (i // per_b, 0))
    return pl.pallas_call(
        _merge_body,
        out_shape=jax.ShapeDtypeStruct((t, d), F32),
        grid=(t // tm,),
        in_specs=[row(d), row(RET_V_W), row(DIFF_V_W), row(MEM_Q_W), memb, memb,
                  _resident(nw.shape), _resident_cols(d, RET_V_W, 2),
                  _resident_cols(d, D_MODEL, QKV_W // D_MODEL),
                  _resident_cols(d, D_MODEL, QKV_W // D_MODEL + 1),
                  _resident_cols(d, D_MODEL, QKV_W // D_MODEL + 2), _resident(bg.shape), _resident(wro.shape),
                  _resident(wdo.shape), _resident(wmo.shape), _resident(wout.shape)],
        out_specs=row(d),
        compiler_params=_params(("parallel",)),
        name="merge",
    )(x1, ro, do, mq, mk, mv, nw, w_in, w_in, w_in, w_in, bg, wro, wdo, wmo, wout)


def _deinterleave_matrix():
    j = jnp.arange(RET_DK)
    src = jnp.where(j < RET_DK // 2, 2 * j, 2 * (j - RET_DK // 2) + 1)
    return (jnp.arange(RET_DK)[:, None] == src[None, :]).astype(BF16)


def _rotation_tables(positions):
    pos = positions.astype(F32).reshape(-1, 1)
    ret_inv = (1.0 / (RET_THETA_BASE ** jnp.linspace(0.0, 1.0, RET_DK // 2, dtype=F32))).reshape(1, -1)
    rope_inv = 1.0 / (ROPE_THETA ** (jnp.arange(0, ROT_DIM, 2, dtype=F32) / ROT_DIM))
    d_ang = rope_inv[:, None] * pos.reshape(1, -1)
    dtab = jnp.concatenate([jnp.cos(d_ang), jnp.sin(d_ang), jnp.zeros((LANES - ROT_DIM, pos.shape[0]), F32)], axis=0)
    return pos, ret_inv, dtab


def _decay_tables():
    log_g = jnp.log(1.0 - 2.0 ** (-5.0 - jnp.arange(RET_HEADS, dtype=F32)))
    idx = jnp.arange(RET_BLOCK, dtype=F32)
    dist = jnp.abs(idx[:, None] - idx[None, :])
    chunk = jnp.arange(RET_BLOCK) // CHUNK
    visible = chunk[None, :] <= chunk[:, None]
    dmat = jnp.where(visible[None], jnp.exp(log_g[:, None, None] * dist[None]), 0.0)
    qdec = jnp.exp(log_g[:, None] * (idx[None, :] + 1.0))[..., None]
    kdec = jnp.exp(log_g[:, None] * (RET_BLOCK - 1.0 - idx[None, :]))[..., None]
    cdec = jnp.broadcast_to(jnp.exp(log_g * RET_BLOCK)[:, None, None], (RET_HEADS, 1, RET_DV))
    return dmat, qdec, kdec, cdec


def kernel(x, mem, positions, ffn1_norm, ffn1_w_gate, ffn1_w_up, ffn1_w_down, mix_norm, w_in, b_gate, ret_w_o, diff_q_norm, diff_k_norm, diff_lambda_q1, diff_lambda_k1, diff_lambda_q2, diff_lambda_k2, diff_subln, diff_w_o, mem_norm, mem_w_kv, mem_q_norm, mem_k_norm, mem_w_o, w_out, ffn2_norm, ffn2_w_gate, ffn2_w_up, ffn2_w_down, final_norm):
    batch, seq, d = x.shape
    mem_len = mem.shape[1]
    assert d == D_MODEL and seq % RET_BLOCK == 0 and seq % DIFF_QBLOCK == 0
    assert w_in.shape[0] == 1, "single-layer trunk"
    vec = lambda a: a.reshape(1, -1)

    pos, ret_inv, dtab = _rotation_tables(positions)
    dmat, qdec, kdec, cdec = _decay_tables()
    perm = _deinterleave_matrix()
    dqn = jnp.tile(vec(diff_q_norm[0]), (1, 2))
    dkn = jnp.tile(vec(diff_k_norm[0]), (1, 2))
    lamv = jnp.stack([diff_lambda_q1[0], diff_lambda_k1[0], diff_lambda_q2[0], diff_lambda_k2[0]]).astype(F32)

    mk, mv, rcos, rsin, (w1g, w1u, w1d) = _memkv(
        mem.reshape(batch * mem_len, d), vec(mem_norm[0]), mem_w_kv[0], vec(mem_k_norm[0]), pos, ret_inv,
        (ffn1_w_gate[0], ffn1_w_up[0], ffn1_w_down[0]))
    x1, w_rqk, (wi, w_ro, w_do, w_mo, w_o) = _ffn_first(
        x.reshape(batch * seq, d), vec(ffn1_norm[0]), w1g, w1u, w1d, perm,
        (w_in[0], ret_w_o[0], diff_w_o[0], mem_w_o[0], w_out[0]))
    (rq, rk, rv, dq, dk, dv, mq), (w2g, w2u, w2d) = _proj(
        x1, vec(mix_norm[0]), w_rqk, wi, rcos, rsin, dtab, dqn, dkn, vec(mem_q_norm[0]),
        (ffn2_w_gate[0], ffn2_w_up[0], ffn2_w_down[0]))
    ro, do = _mixers(rq, rk, rv, dmat, qdec, kdec, cdec, dq, dk, dv, lamv, vec(diff_subln[0]), batch=batch, seq=seq)
    x2 = _merge(x1, ro, do, mq, mk, mv, vec(mix_norm[0]), wi, vec(b_gate[0]), w_ro, w_do, w_mo, w_o,
                seq=seq, mem_len=mem_len)
    x3 = _ffn_last(x2, vec(ffn2_norm[0]), w2g, w2u, w2d, vec(final_norm[0]))
    return x3.reshape(batch, seq, d)
```

```python
import functools
import math

import jax
import jax.numpy as jnp
from jax import lax
from jax.experimental import pallas as pl
from jax.experimental.pallas import tpu as pltpu

D_MODEL = 1024
CHUNK = 64
EPS = 1e-6
RET_HEADS = 4
RET_DK = 256
RET_DV = 512
RET_THETA_BASE = 10000.0
DIFF_HEADS = 8
DIFF_DK = 64
DIFF_DV = 128
ROPE_THETA = 500000.0
ROT_DIM = 16
MEM_HEADS = 4
MEM_DH = 256
D_FF = 2816
LAM_INIT = 0.8 - 0.6 * math.exp(-0.3 * 0)
LOG2E = math.log2(math.e)

RET_QK_W = RET_HEADS * RET_DK
RET_V_W = RET_HEADS * RET_DV
DIFF_QK_W = DIFF_HEADS * 2 * DIFF_DK
DIFF_V_W = DIFF_HEADS * DIFF_DV
MEM_Q_W = MEM_HEADS * MEM_DH
QKV_W = 2 * RET_QK_W + 2 * RET_V_W + 2 * DIFF_QK_W + DIFF_V_W + MEM_Q_W
GATE_W = 3 * D_MODEL

LANES = 128
BF16_SUBLANES = 16
FFN_SUBROWS = 256
PROJ_SUBROWS = 256
MERGE_SUBROWS = 256
RET_BLOCK = 256
RET_HEADS_PER_STEP = 1
DIFF_QBLOCK = 256
DIFF_HEADS_PER_STEP = 2
DIFF_LOOKAHEAD = 2
VMEM_LIMIT = 56 * 1024 * 1024

BF16 = jnp.bfloat16
F32 = jnp.float32


def _dot(a, b):
    return jnp.dot(a, b, preferred_element_type=F32)


def _dot_nt(a, b):
    return lax.dot_general(a, b, (((1,), (1,)), ((), ())), preferred_element_type=F32)


def _dot_tn(a, b):
    return lax.dot_general(a, b, (((0,), (0,)), ((), ())), preferred_element_type=F32)


def _rms(x, w=None):
    y = x * lax.rsqrt(jnp.mean(x * x, axis=-1, keepdims=True) + EPS)
    if w is not None:
        y = y * w
    return y


def _resident(shape):
    nd = len(shape)
    return pl.BlockSpec(shape, lambda *_: (0,) * nd, pipeline_mode=pl.Buffered(1))


def _resident_cols(rows, width, index):
    return pl.BlockSpec((rows, width), lambda *_: (0, index), pipeline_mode=pl.Buffered(1))


def _params(sem):
    return pltpu.CompilerParams(dimension_semantics=sem, vmem_limit_bytes=VMEM_LIMIT)


def _swiglu_half_step(x, nw_ref, wg_ref, wu_ref, wd_ref):
    h = _rms(x, nw_ref[...]).astype(BF16)
    g = _dot(h, wg_ref[...])
    u = _dot(h, wu_ref[...])
    a = (g * jax.nn.sigmoid(g) * u).astype(BF16)
    return x + 0.5 * _dot(a, wd_ref[...])


def _cast_chunks(src_refs, dst_refs):
    for src, dst in zip(src_refs, dst_refs):
        dst[...] = src[...].astype(BF16)


def _cast_specs(mats, steps):
    specs, shapes = [], []
    for m in mats:
        rows, cols = m.shape
        chunk = next(c for c in range(BF16_SUBLANES, rows + 1, BF16_SUBLANES)
                     if rows % c == 0 and rows // c <= steps)
        last = rows // chunk - 1
        specs.append(pl.BlockSpec((chunk, cols), lambda i, last=last: (jnp.minimum(i, last), 0)))
        shapes.append(jax.ShapeDtypeStruct((rows, cols), BF16))
    return specs, shapes


def _ffn_first_body(x_ref, nw_ref, wg_ref, wu_ref, wd_ref, perm_ref, *rest, n_cast):
    cast_in, (o_ref, wrqk_ref), cast_out = rest[:n_cast], rest[n_cast:n_cast + 2], rest[n_cast + 2:]
    _cast_chunks(cast_in, cast_out)
    perm = perm_ref[...]
    for hb in range(2 * RET_HEADS):
        cols = slice(hb * RET_DK, (hb + 1) * RET_DK)
        wrqk_ref[:, cols] = _dot(cast_in[0][:, cols].astype(BF16), perm).astype(BF16)
    for r in range(0, x_ref.shape[0], FFN_SUBROWS):
        rows = slice(r, r + FFN_SUBROWS)
        o_ref[rows, :] = _swiglu_half_step(x_ref[rows, :], nw_ref, wg_ref, wu_ref, wd_ref)


def _ffn_first(x, nw, wg, wu, wd, perm, cast, *, tm=1024):
    t, d = x.shape
    steps = t // tm
    row = pl.BlockSpec((tm, d), lambda i: (i, 0))
    cast_specs, cast_shapes = _cast_specs(cast, steps)
    w_in_chunk = cast_specs[0].block_shape[0]
    assert cast[0].shape[0] == w_in_chunk * steps, "every step must see a fresh row chunk of w_in"
    wrqk_spec = pl.BlockSpec((w_in_chunk, 2 * RET_QK_W), lambda i: (i, 0))
    outs = pl.pallas_call(
        functools.partial(_ffn_first_body, n_cast=len(cast)),
        out_shape=(jax.ShapeDtypeStruct((t, d), F32),
                   jax.ShapeDtypeStruct((cast[0].shape[0], 2 * RET_QK_W), BF16), *cast_shapes),
        grid=(steps,),
        in_specs=[row, _resident(nw.shape), _resident(wg.shape), _resident(wu.shape), _resident(wd.shape),
                  _resident(perm.shape), *cast_specs],
        out_specs=(row, wrqk_spec, *cast_specs),
        compiler_params=_params(("arbitrary",)),
        name="ffn_first",
    )(x, nw, wg, wu, wd, perm, *cast)
    return outs[0], outs[1], outs[2:]


def _ffn_last_body(x_ref, nw_ref, wg_ref, wu_ref, wd_ref, fw_ref, o_ref):
    for r in range(0, x_ref.shape[0], FFN_SUBROWS):
        rows = slice(r, r + FFN_SUBROWS)
        o_ref[rows, :] = _rms(_swiglu_half_step(x_ref[rows, :], nw_ref, wg_ref, wu_ref, wd_ref), fw_ref[...])


def _ffn_last(x, nw, wg, wu, wd, fw, *, tm=1024):
    t, d = x.shape
    row = pl.BlockSpec((tm, d), lambda i: (i, 0))
    return pl.pallas_call(
        _ffn_last_body,
        out_shape=jax.ShapeDtypeStruct((t, d), F32),
        grid=(t // tm,),
        in_specs=[row, _resident(nw.shape), _resident(wg.shape), _resident(wu.shape),
                  _resident(wd.shape), _resident(fw.shape)],
        out_specs=row,
        compiler_params=_params(("parallel",)),
        name="ffn_last",
    )(x, nw, wg, wu, wd, fw)


def _proj_body(x_ref, nw_ref, wrqk_ref, wrv_ref, wdq_ref, wdk_ref, wdv_ref, wmq_ref,
               rcos_ref, rsin_ref, dtab_ref, dqn_ref, dkn_ref, mqn_ref, *rest, n_cast):
    cast_in, cast_out = rest[:n_cast], rest[n_cast + 7:]
    rq_ref, rk_ref, rv_ref, dq_ref, dk_ref, dv_ref, mq_ref = rest[n_cast:n_cast + 7]
    lane = lax.broadcasted_iota(jnp.int32, (1, LANES), 1)
    half = ROT_DIM // 2
    low = lane < DIFF_DK
    first = (lane % DIFF_DK) < half
    for r in range(0, x_ref.shape[0], PROJ_SUBROWS):
        _proj_rows(slice(r, r + PROJ_SUBROWS), lane, half, low, first, x_ref, nw_ref, wrqk_ref, wrv_ref, wdq_ref,
                   wdk_ref, wdv_ref, wmq_ref, rcos_ref, rsin_ref, dtab_ref, dqn_ref, dkn_ref, mqn_ref,
                   rq_ref, rk_ref, rv_ref, dq_ref, dk_ref, dv_ref, mq_ref)
    _cast_chunks(cast_in, cast_out)


def _proj_rows(rows, lane, half, low, first, x_ref, nw_ref, wrqk_ref, wrv_ref, wdq_ref, wdk_ref, wdv_ref, wmq_ref,
               rcos_ref, rsin_ref, dtab_ref, dqn_ref, dkn_ref, mqn_ref,
               rq_ref, rk_ref, rv_ref, dq_ref, dk_ref, dv_ref, mq_ref):
    h = _rms(x_ref[rows, :], nw_ref[...]).astype(BF16)
    rcos = rcos_ref[rows, :]
    rsin = rsin_ref[rows, :]

    dtab = dtab_ref[:, rows].T
    c = jnp.where(lane < half, dtab, 0.0)
    c = c + pltpu.roll(c, half, 1)
    c = c + pltpu.roll(c, DIFF_DK, 1)
    dcos = jnp.where((lane % DIFF_DK) < ROT_DIM, c, 1.0)
    sn = jnp.where((lane >= half) & (lane < ROT_DIM), dtab, 0.0)
    sn = sn - pltpu.roll(sn, LANES - half, 1)
    dsin = sn + pltpu.roll(sn, DIFF_DK, 1)

    def ret_rotate(out_ref, scale):
        def epilogue(y):
            for hd in range(RET_HEADS):
                ye = y[:, hd * RET_DK:hd * RET_DK + LANES]
                yo = y[:, hd * RET_DK + LANES:(hd + 1) * RET_DK]
                out_ref[rows, hd * RET_DK:hd * RET_DK + LANES] = ((ye * rcos - yo * rsin) * scale).astype(BF16)
                out_ref[rows, hd * RET_DK + LANES:(hd + 1) * RET_DK] = ((yo * rcos + ye * rsin) * scale).astype(BF16)
        return epilogue

    def store(out_ref):
        def epilogue(y):
            out_ref[rows, :] = y.astype(BF16)
        return epilogue

    def diff_norm_rope(out_ref, n_ref, scale):
        def epilogue(y):
            gain = n_ref[...]
            for hd in range(DIFF_HEADS):
                yh = y[:, hd * LANES:(hd + 1) * LANES]
                sq = yh * yh
                ss_lo = jnp.sum(jnp.where(low, sq, 0.0), axis=-1, keepdims=True)
                ss_hi = jnp.sum(jnp.where(low, 0.0, sq), axis=-1, keepdims=True)
                r = jnp.where(low, lax.rsqrt(ss_lo * (1.0 / DIFF_DK) + EPS),
                              lax.rsqrt(ss_hi * (1.0 / DIFF_DK) + EPS))
                yn = yh * r * gain
                partner = jnp.where(first, pltpu.roll(yn, LANES - half, 1), pltpu.roll(yn, half, 1))
                out_ref[rows, hd * LANES:(hd + 1) * LANES] = ((yn * dcos + partner * dsin) * scale).astype(BF16)
        return epilogue

    def mem_norm(y):
        gain = mqn_ref[...]
        for hd in range(MEM_HEADS):
            yh = y[:, hd * MEM_DH:(hd + 1) * MEM_DH]
            mq_ref[rows, hd * MEM_DH:(hd + 1) * MEM_DH] = (_rms(yh) * gain).astype(BF16)

    stages = (
        (lambda: wdq_ref[...], diff_norm_rope(dq_ref, dqn_ref, LOG2E * DIFF_DK ** -0.5)),
        (lambda: wdk_ref[...], diff_norm_rope(dk_ref, dkn_ref, 1.0)),
        (lambda: wmq_ref[...], mem_norm),
        (lambda: wrqk_ref[:, 0:RET_QK_W], ret_rotate(rq_ref, 1.0)),
        (lambda: wrqk_ref[:, RET_QK_W:2 * RET_QK_W], ret_rotate(rk_ref, RET_DK ** -0.5)),
        (lambda: wdv_ref[...], store(dv_ref)),
        (lambda: wrv_ref[...], store(rv_ref)),
    )
    y_next = _dot(h, stages[0][0]())
    for i, (_, epilogue) in enumerate(stages):
        y = y_next
        if i + 1 < len(stages):
            y_next = _dot(h, stages[i + 1][0]())
        epilogue(y)


def _proj(x1, nw, w_rqk, w_in, rcos, rsin, dtab, dqn, dkn, mqn, cast, *, tm=512):
    t, d = x1.shape
    cast_specs, cast_shapes = _cast_specs(cast, t // tm)
    widths = (RET_QK_W, RET_QK_W, RET_V_W, DIFF_QK_W, DIFF_QK_W, DIFF_V_W, MEM_Q_W)

    def row(w):
        return pl.BlockSpec((tm, w), lambda i: (i, 0))

    outs = pl.pallas_call(
        functools.partial(_proj_body, n_cast=len(cast)),
        out_shape=(*(jax.ShapeDtypeStruct((t, w), BF16) for w in widths), *cast_shapes),
        grid=(t // tm,),
        in_specs=[row(d), _resident(nw.shape), _resident(w_rqk.shape),
                  _resident_cols(d, RET_V_W, 1),
                  _resident_cols(d, DIFF_QK_W, 6), _resident_cols(d, DIFF_QK_W, 7),
                  _resident_cols(d, DIFF_V_W, 8), _resident_cols(d, MEM_Q_W, 9),
                  row(LANES), row(LANES), pl.BlockSpec((LANES, tm), lambda i: (0, i)),
                  _resident(dqn.shape), _resident(dkn.shape), _resident(mqn.shape), *cast_specs],
        out_specs=(*(row(w) for w in widths), *cast_specs),
        compiler_params=_params(("arbitrary",)),
        name="proj",
    )(x1, nw, w_rqk, w_in, w_in, w_in, w_in, w_in, rcos, rsin, dtab, dqn, dkn, mqn, *cast)
    return outs[:len(widths)], outs[len(widths):]


def _memkv_body(m_ref, nw_ref, w_ref, kn_ref, pos_ref, rinv_ref, *rest, n_cast):
    cast_in, (mk_ref, mv_ref, rcos_ref, rsin_ref), cast_out = (
        rest[:n_cast], rest[n_cast:n_cast + 4], rest[n_cast + 4:])
    _cast_chunks(cast_in, cast_out)
    ang = pos_ref[...] * rinv_ref[...]
    rcos_ref[...] = jnp.cos(ang)
    rsin_ref[...] = jnp.sin(ang)
    hm = _rms(m_ref[...], nw_ref[...]).astype(BF16)
    k = _dot(hm, w_ref[:, 0:MEM_Q_W].astype(BF16))
    gain = kn_ref[...] * (MEM_DH ** -0.5)
    for hd in range(MEM_HEADS):
        kh = k[:, hd * MEM_DH:(hd + 1) * MEM_DH]
        mk_ref[:, hd * MEM_DH:(hd + 1) * MEM_DH] = (_rms(kh) * gain).astype(BF16)
    mv_ref[...] = _dot(hm, w_ref[:, MEM_Q_W:2 * MEM_Q_W].astype(BF16)).astype(BF16)


def _memkv(mem2d, nw, w_kv, kn, pos, rinv, cast, *, tm=256):
    t, d = mem2d.shape
    steps = t // tm
    tok = pos.shape[0] // steps
    row = pl.BlockSpec((tm, d), lambda i: (i, 0))
    out = pl.BlockSpec((tm, MEM_Q_W), lambda i: (i, 0))
    tab = pl.BlockSpec((tok, LANES), lambda i: (i, 0))
    cast_specs, cast_shapes = _cast_specs(cast, steps)
    outs = pl.pallas_call(
        functools.partial(_memkv_body, n_cast=len(cast)),
        out_shape=(jax.ShapeDtypeStruct((t, MEM_Q_W), BF16),) * 2
        + (jax.ShapeDtypeStruct((pos.shape[0], LANES), F32),) * 2 + tuple(cast_shapes),
        grid=(steps,),
        in_specs=[row, _resident(nw.shape), _resident(w_kv.shape), _resident(kn.shape),
                  pl.BlockSpec((tok, 1), lambda i: (i, 0)), _resident(rinv.shape), *cast_specs],
        out_specs=(out, out, tab, tab, *cast_specs),
        compiler_params=_params(("arbitrary",)),
        name="memkv",
    )(mem2d, nw, w_kv, kn, pos, rinv, *cast)
    return outs[0], outs[1], outs[2], outs[3], outs[4:]


def _ret_body(q_ref, k_ref, v_ref, dmat_ref, qdec_ref, kdec_ref, cdec_ref, o_ref, state_ref, *, seq):
    nblk = seq // RET_BLOCK

    def local(item):
        n, hd = item
        rows = slice(n * RET_BLOCK, (n + 1) * RET_BLOCK)
        q = q_ref[rows, hd * RET_DK:(hd + 1) * RET_DK]
        k = k_ref[rows, hd * RET_DK:(hd + 1) * RET_DK]
        v = v_ref[rows, hd * RET_DV:(hd + 1) * RET_DV]
        s = _dot_nt(q, k) * dmat_ref[hd]
        o = _dot(s.astype(BF16), v)
        upd = _dot_tn((k.astype(F32) * kdec_ref[hd]).astype(BF16), v)
        qd = (q.astype(F32) * qdec_ref[hd]).astype(BF16) if n else None
        return o, upd, qd

    items = [(n, hd) for n in range(nblk) for hd in range(RET_HEADS_PER_STEP)]
    nxt = local(items[0])
    for step, (n, hd) in enumerate(items):
        rows = slice(n * RET_BLOCK, (n + 1) * RET_BLOCK)
        o, upd, qd = nxt
        if step + 1 < len(items):
            nxt = local(items[step + 1])
        if n == 0:
            state_ref[hd] = upd
        else:
            st = state_ref[hd]
            o = o + _dot(qd, st.astype(BF16))
            state_ref[hd] = st * cdec_ref[hd] + upd
        o_ref[rows, hd * RET_DV:(hd + 1) * RET_DV] = _rms(o).astype(BF16)


def _diff_body(q_ref, k_ref, v_ref, lamv_ref, subln_ref, o_ref, vx_ref, *, seq):
    lv = lamv_ref[...]
    lam = (jnp.exp(jnp.sum(lv[0:1] * lv[1:2], axis=-1, keepdims=True))
           - jnp.exp(jnp.sum(lv[2:3] * lv[3:4], axis=-1, keepdims=True)) + LAM_INIT)
    gain = subln_ref[...] * (1.0 - LAM_INIT)
    tq = DIFF_QBLOCK
    low = lax.broadcasted_iota(jnp.int32, (1, LANES), 1) < DIFF_DK
    rchunk = (lax.broadcasted_iota(jnp.int32, (2 * tq, tq), 0) % tq) // CHUNK
    cchunk = lax.broadcasted_iota(jnp.int32, (2 * tq, tq), 1) // CHUNK
    visible = cchunk <= rchunk
    zero = jnp.zeros((), BF16)

    nq = seq // tq
    for hd in range(DIFF_HEADS_PER_STEP):
        vx_ref[hd, :, 0:DIFF_DV] = v_ref[:, hd * DIFF_DV:(hd + 1) * DIFF_DV]
        vx_ref[hd, :, DIFF_DV:2 * DIFF_DV] = jnp.ones((seq, DIFF_DV), BF16)

    def scores(item):
        hd, qi = item
        cols = slice(hd * LANES, (hd + 1) * LANES)
        q = q_ref[qi * tq:(qi + 1) * tq, cols]
        q2 = jnp.concatenate([jnp.where(low, q, zero), jnp.where(low, zero, q)], axis=0)
        sd = jnp.where(visible, _dot_nt(q2, k_ref[qi * tq:(qi + 1) * tq, cols]), -jnp.inf)
        sp = _dot_nt(q2, k_ref[0:qi * tq, cols]) if qi else None
        return sd, sp

    items = [(hd, qi) for hd in range(DIFF_HEADS_PER_STEP) for qi in reversed(range(nq))]
    ahead = [scores(item) for item in items[:DIFF_LOOKAHEAD]]
    for step, (hd, qi) in enumerate(items):
        rows = slice(qi * tq, (qi + 1) * tq)
        past = qi * tq
        sd, sp = ahead.pop(0)
        if step + DIFF_LOOKAHEAD < len(items):
            ahead.append(scores(items[step + DIFF_LOOKAHEAD]))
        m = jnp.max(sd, axis=-1, keepdims=True)
        if past:
            m = jnp.maximum(m, jnp.max(sp, axis=-1, keepdims=True))
        ol = _dot(jnp.exp2(sd - m).astype(BF16), vx_ref[hd, rows, :])
        if past:
            ol = ol + _dot(jnp.exp2(sp - m).astype(BF16), vx_ref[hd, 0:past, :])
        o0 = ol[0:tq, 0:DIFF_DV] / ol[0:tq, DIFF_DV:2 * DIFF_DV]
        o1 = ol[tq:2 * tq, 0:DIFF_DV] / ol[tq:2 * tq, DIFF_DV:2 * DIFF_DV]
        o_ref[rows, hd * DIFF_DV:(hd + 1) * DIFF_DV] = (_rms(o0 - lam * o1) * gain).astype(BF16)


def _mixers_body(rq_ref, rk_ref, rv_ref, dmat_ref, qdec_ref, kdec_ref, cdec_ref,
                 dq_ref, dk_ref, dv_ref, lamv_ref, subln_ref, ro_ref, do_ref, state_ref, vx_ref, *, seq):
    _ret_body(rq_ref, rk_ref, rv_ref, dmat_ref, qdec_ref, kdec_ref, cdec_ref, ro_ref, state_ref, seq=seq)
    _diff_body(dq_ref, dk_ref, dv_ref, lamv_ref, subln_ref, do_ref, vx_ref, seq=seq)


def _mixers(rq, rk, rv, dmat, qdec, kdec, cdec, dq, dk, dv, lamv, subln, *, batch, seq):
    steps = RET_HEADS // RET_HEADS_PER_STEP
    assert steps == DIFF_HEADS // DIFF_HEADS_PER_STEP
    qk = pl.BlockSpec((seq, RET_HEADS_PER_STEP * RET_DK), lambda b, h: (b, h))
    vv = pl.BlockSpec((seq, RET_HEADS_PER_STEP * RET_DV), lambda b, h: (b, h))
    blk = pl.BlockSpec((seq, DIFF_HEADS_PER_STEP * LANES), lambda b, h: (b, h))

    def per_head(shape):
        return pl.BlockSpec((RET_HEADS_PER_STEP,) + shape[1:], lambda b, h: (h, 0, 0))

    return pl.pallas_call(
        functools.partial(_mixers_body, seq=seq),
        out_shape=(jax.ShapeDtypeStruct(rv.shape, BF16), jax.ShapeDtypeStruct(dv.shape, BF16)),
        grid=(batch, steps),
        in_specs=[qk, qk, vv, per_head(dmat.shape), per_head(qdec.shape), per_head(kdec.shape),
                  per_head(cdec.shape), blk, blk, blk, _resident(lamv.shape), _resident(subln.shape)],
        out_specs=(vv, blk),
        scratch_shapes=[pltpu.VMEM((RET_HEADS_PER_STEP, RET_DK, RET_DV), F32),
                        pltpu.VMEM((DIFF_HEADS_PER_STEP, seq, 2 * DIFF_DV), BF16)],
        compiler_params=_params(("parallel", "parallel")),
        name="mixers",
    )(rq, rk, rv, dmat, qdec, kdec, cdec, dq, dk, dv, lamv, subln)


def _merge_body(x_ref, ro_ref, do_ref, mq_ref, mk_ref, mv_ref, nw_ref, wrg_ref, wg0_ref, wg1_ref, wg2_ref, bg_ref,
                wro_ref, wdo_ref, wmo_ref, wout_ref, o_ref):
    for r in range(0, x_ref.shape[0], MERGE_SUBROWS):
        rows = slice(r, r + MERGE_SUBROWS)
        heads = [slice(hd * MEM_DH, (hd + 1) * MEM_DH) for hd in range(MEM_HEADS)]
        scores = [_dot_nt(mq_ref[rows, cols], mk_ref[:, cols]) for cols in heads]
        x = x_ref[rows, :]
        h = _rms(x, nw_ref[...]).astype(BF16)
        rg = _dot(h, wrg_ref[...])
        probs = []
        for s in scores:
            p = jnp.exp(s - jnp.max(s, axis=-1, keepdims=True))
            probs.append((p * (1.0 / jnp.sum(p, axis=-1, keepdims=True))).astype(BF16))
        g = [jax.nn.sigmoid(_dot(h, w_ref[...]) + bg_ref[:, i * D_MODEL:(i + 1) * D_MODEL])
             for i, w_ref in enumerate((wg0_ref, wg1_ref, wg2_ref))]
        mem_heads = [_dot(p, mv_ref[:, cols]).astype(BF16) for p, cols in zip(probs, heads)]
        ro = (ro_ref[rows, :].astype(F32) * (rg * jax.nn.sigmoid(rg))).astype(BF16)
        ret_out = _dot(ro, wro_ref[...])
        mem_out = None
        for oh, cols in zip(mem_heads, heads):
            part = _dot(oh, wmo_ref[cols, :])
            mem_out = part if mem_out is None else mem_out + part
        diff_out = _dot(do_ref[rows, :], wdo_ref[...])
        merged = g[0] * ret_out + g[1] * diff_out + g[2] * mem_out
        o_ref[rows, :] = x + _dot(merged.astype(BF16), wout_ref[...])


def _merge(x1, ro, do, mq, mk, mv, nw, w_in, bg, wro, wdo, wmo, wout, *, seq, mem_len, tm=512):
    t, d = x1.shape
    per_b = seq // tm

    def row(w):
        return pl.BlockSpec((tm, w), lambda i: (i, 0))

    memb = pl.BlockSpec((mem_len, MEM_Q_W), lambda i: (i // per_b, 0))
    return pl.pallas_call(
        _merge_body,
        out_shape=jax.ShapeDtypeStruct((t, d), F32),
        grid=(t // tm,),
        in_specs=[row(d), row(RET_V_W), row(DIFF_V_W), row(MEM_Q_W), memb, memb,
                  _resident(nw.shape), _resident_cols(d, RET_V_W, 2),
                  _resident_cols(d, D_MODEL, QKV_W // D_MODEL),
                  _resident_cols(d, D_MODEL, QKV_W // D_MODEL + 1),
                  _resident_cols(d, D_MODEL, QKV_W // D_MODEL + 2), _resident(bg.shape), _resident(wro.shape),
                  _resident(wdo.shape), _resident(wmo.shape), _resident(wout.shape)],
        out_specs=row(d),
        compiler_params=_params(("parallel",)),
        name="merge",
    )(x1, ro, do, mq, mk, mv, nw, w_in, w_in, w_in, w_in, bg, wro, wdo, wmo, wout)


def _deinterleave_matrix():
    j = jnp.arange(RET_DK)
    src = jnp.where(j < RET_DK // 2, 2 * j, 2 * (j - RET_DK // 2) + 1)
    return (jnp.arange(RET_DK)[:, None] == src[None, :]).astype(BF16)


def _rotation_tables(positions):
    pos = positions.astype(F32).reshape(-1, 1)
    ret_inv = (1.0 / (RET_THETA_BASE ** jnp.linspace(0.0, 1.0, RET_DK // 2, dtype=F32))).reshape(1, -1)
    rope_inv = 1.0 / (ROPE_THETA ** (jnp.arange(0, ROT_DIM, 2, dtype=F32) / ROT_DIM))
    d_ang = rope_inv[:, None] * pos.reshape(1, -1)
    dtab = jnp.concatenate([jnp.cos(d_ang), jnp.sin(d_ang), jnp.zeros((LANES - ROT_DIM, pos.shape[0]), F32)], axis=0)
    return pos, ret_inv, dtab


def _decay_tables():
    log_g = jnp.log(1.0 - 2.0 ** (-5.0 - jnp.arange(RET_HEADS, dtype=F32)))
    idx = jnp.arange(RET_BLOCK, dtype=F32)
    dist = jnp.abs(idx[:, None] - idx[None, :])
    chunk = jnp.arange(RET_BLOCK) // CHUNK
    visible = chunk[None, :] <= chunk[:, None]
    dmat = jnp.where(visible[None], jnp.exp(log_g[:, None, None] * dist[None]), 0.0)
    qdec = jnp.exp(log_g[:, None] * (idx[None, :] + 1.0))[..., None]
    kdec = jnp.exp(log_g[:, None] * (RET_BLOCK - 1.0 - idx[None, :]))[..., None]
    cdec = jnp.broadcast_to(jnp.exp(log_g * RET_BLOCK)[:, None, None], (RET_HEADS, 1, RET_DV))
    return dmat, qdec, kdec, cdec


def kernel(x, mem, positions, ffn1_norm, ffn1_w_gate, ffn1_w_up, ffn1_w_down, mix_norm, w_in, b_gate, ret_w_o, diff_q_norm, diff_k_norm, diff_lambda_q1, diff_lambda_k1, diff_lambda_q2, diff_lambda_k2, diff_subln, diff_w_o, mem_norm, mem_w_kv, mem_q_norm, mem_k_norm, mem_w_o, w_out, ffn2_norm, ffn2_w_gate, ffn2_w_up, ffn2_w_down, final_norm):
    batch, seq, d = x.shape
    mem_len = mem.shape[1]
    assert d == D_MODEL and seq % RET_BLOCK == 0 and seq % DIFF_QBLOCK == 0
    assert w_in.shape[0] == 1, "single-layer trunk"
    vec = lambda a: a.reshape(1, -1)

    pos, ret_inv, dtab = _rotation_tables(positions)
    dmat, qdec, kdec, cdec = _decay_tables()
    perm = _deinterleave_matrix()
    dqn = jnp.tile(vec(diff_q_norm[0]), (1, 2))
    dkn = jnp.tile(vec(diff_k_norm[0]), (1, 2))
    lamv = jnp.stack([diff_lambda_q1[0], diff_lambda_k1[0], diff_lambda_q2[0], diff_lambda_k2[0]]).astype(F32)

    mk, mv, rcos, rsin, (w1g, w1u, w1d) = _memkv(
        mem.reshape(batch * mem_len, d), vec(mem_norm[0]), mem_w_kv[0], vec(mem_k_norm[0]), pos, ret_inv,
        (ffn1_w_gate[0], ffn1_w_up[0], ffn1_w_down[0]))
    x1, w_rqk, (wi, w_ro, w_do, w_mo, w_o) = _ffn_first(
        x.reshape(batch * seq, d), vec(ffn1_norm[0]), w1g, w1u, w1d, perm,
        (w_in[0], ret_w_o[0], diff_w_o[0], mem_w_o[0], w_out[0]))
    (rq, rk, rv, dq, dk, dv, mq), (w2g, w2u, w2d) = _proj(
        x1, vec(mix_norm[0]), w_rqk, wi, rcos, rsin, dtab, dqn, dkn, vec(mem_q_norm[0]),
        (ffn2_w_gate[0], ffn2_w_up[0], ffn2_w_down[0]))
    ro, do = _mixers(rq, rk, rv, dmat, qdec, kdec, cdec, dq, dk, dv, lamv, vec(diff_subln[0]), batch=batch, seq=seq)
    x2 = _merge(x1, ro, do, mq, mk, mv, vec(mix_norm[0]), wi, vec(b_gate[0]), w_ro, w_do, w_mo, w_o,
                seq=seq, mem_len=mem_len)
    x3 = _ffn_last(x2, vec(ffn2_norm[0]), w2g, w2u, w2d, vec(final_norm[0]))
    return x3.reshape(batch, seq, d)
```

```python
import functools
import math

import jax
import jax.numpy as jnp
from jax import lax
from jax.experimental import pallas as pl
from jax.experimental.pallas import tpu as pltpu

D_MODEL = 1024
CHUNK = 64
EPS = 1e-6
RET_HEADS = 4
RET_DK = 256
RET_DV = 512
RET_THETA_BASE = 10000.0
DIFF_HEADS = 8
DIFF_DK = 64
DIFF_DV = 128
ROPE_THETA = 500000.0
ROT_DIM = 16
MEM_HEADS = 4
MEM_DH = 256
LAM_INIT = 0.8 - 0.6 * math.exp(-0.3 * 0)
LOG2E = math.log2(math.e)

RET_QK_W = RET_HEADS * RET_DK
RET_V_W = RET_HEADS * RET_DV
DIFF_QK_W = DIFF_HEADS * 2 * DIFF_DK
DIFF_V_W = DIFF_HEADS * DIFF_DV
MEM_Q_W = MEM_HEADS * MEM_DH
QKV_W = 2 * RET_QK_W + 2 * RET_V_W + 2 * DIFF_QK_W + DIFF_V_W + MEM_Q_W

LANES = 128
BF16_SUBLANES = 16
FFN_SUBROWS = 256
PROJ_SUBROWS = 256
MERGE_SUBROWS = 256
RET_BLOCK = 256
RET_HEADS_PER_STEP = 1
DIFF_QBLOCK = 256
DIFF_HEADS_PER_STEP = 2
DIFF_LOOKAHEAD = 2
VMEM_LIMIT = 56 * 1024 * 1024

BF16 = jnp.bfloat16
F32 = jnp.float32


def _dot(a, b):
    return jnp.dot(a, b, preferred_element_type=F32)


def _dot_nt(a, b):
    return lax.dot_general(a, b, (((1,), (1,)), ((), ())), preferred_element_type=F32)


def _dot_tn(a, b):
    return lax.dot_general(a, b, (((0,), (0,)), ((), ())), preferred_element_type=F32)


def _rms(x, w=None):
    y = x * lax.rsqrt(jnp.mean(x * x, axis=-1, keepdims=True) + EPS)
    if w is not None:
        y = y * w
    return y


def _resident(shape):
    nd = len(shape)
    return pl.BlockSpec(shape, lambda *_: (0,) * nd, pipeline_mode=pl.Buffered(1))


def _resident_cols(rows, width, index):
    return pl.BlockSpec((rows, width), lambda *_: (0, index), pipeline_mode=pl.Buffered(1))


def _params(sem):
    return pltpu.CompilerParams(dimension_semantics=sem, vmem_limit_bytes=VMEM_LIMIT)


def _swiglu_rows(x_ref, nw_ref, wg_ref, wu_ref, wd_ref, finish):
    def activation(rows):
        x = x_ref[rows, :]
        h = _rms(x, nw_ref[...]).astype(BF16)
        g = _dot(h, wg_ref[...])
        u = _dot(h, wu_ref[...])
        return x, (g * jax.nn.sigmoid(g) * u).astype(BF16)

    blocks = [slice(r, r + FFN_SUBROWS) for r in range(0, x_ref.shape[0], FFN_SUBROWS)]
    nxt = activation(blocks[0])
    for i, rows in enumerate(blocks):
        x, a = nxt
        if i + 1 < len(blocks):
            nxt = activation(blocks[i + 1])
        finish(rows, x + 0.5 * _dot(a, wd_ref[...]))


def _cast_chunks(src_refs, dst_refs):
    for src, dst in zip(src_refs, dst_refs):
        dst[...] = src[...].astype(BF16)


def _cast_specs(mats, steps):
    specs, shapes = [], []
    for m in mats:
        rows, cols = m.shape
        chunk = next(c for c in range(BF16_SUBLANES, rows + 1, BF16_SUBLANES)
                     if rows % c == 0 and rows // c <= steps)
        last = rows // chunk - 1
        specs.append(pl.BlockSpec((chunk, cols), lambda i, last=last: (jnp.minimum(i, last), 0)))
        shapes.append(jax.ShapeDtypeStruct((rows, cols), BF16))
    return specs, shapes


def _ffn_first_body(x_ref, nw_ref, wg_ref, wu_ref, wd_ref, perm_ref, *rest, n_cast):
    cast_in, (o_ref, wrqk_ref), cast_out = rest[:n_cast], rest[n_cast:n_cast + 2], rest[n_cast + 2:]
    _cast_chunks(cast_in, cast_out)
    perm = perm_ref[...]
    for hb in range(2 * RET_HEADS):
        cols = slice(hb * RET_DK, (hb + 1) * RET_DK)
        wrqk_ref[:, cols] = _dot(cast_in[0][:, cols].astype(BF16), perm).astype(BF16)

    def finish(rows, y):
        o_ref[rows, :] = y

    _swiglu_rows(x_ref, nw_ref, wg_ref, wu_ref, wd_ref, finish)


def _ffn_first(x, nw, wg, wu, wd, perm, cast, *, tm=1024):
    t, d = x.shape
    steps = t // tm
    row = pl.BlockSpec((tm, d), lambda i: (i, 0))
    cast_specs, cast_shapes = _cast_specs(cast, steps)
    w_in_chunk = cast_specs[0].block_shape[0]
    assert cast[0].shape[0] == w_in_chunk * steps, "every step must see a fresh row chunk of w_in"
    wrqk_spec = pl.BlockSpec((w_in_chunk, 2 * RET_QK_W), lambda i: (i, 0))
    outs = pl.pallas_call(
        functools.partial(_ffn_first_body, n_cast=len(cast)),
        out_shape=(jax.ShapeDtypeStruct((t, d), F32),
                   jax.ShapeDtypeStruct((cast[0].shape[0], 2 * RET_QK_W), BF16), *cast_shapes),
        grid=(steps,),
        in_specs=[row, _resident(nw.shape), _resident(wg.shape), _resident(wu.shape), _resident(wd.shape),
                  _resident(perm.shape), *cast_specs],
        out_specs=(row, wrqk_spec, *cast_specs),
        compiler_params=_params(("arbitrary",)),
        name="ffn_first",
    )(x, nw, wg, wu, wd, perm, *cast)
    return outs[0], outs[1], outs[2:]


def _ffn_last_body(x_ref, nw_ref, wg_ref, wu_ref, wd_ref, fw_ref, o_ref):
    def finish(rows, y):
        o_ref[rows, :] = _rms(y, fw_ref[...])

    _swiglu_rows(x_ref, nw_ref, wg_ref, wu_ref, wd_ref, finish)


def _ffn_last(x, nw, wg, wu, wd, fw, *, tm=1024):
    t, d = x.shape
    row = pl.BlockSpec((tm, d), lambda i: (i, 0))
    return pl.pallas_call(
        _ffn_last_body,
        out_shape=jax.ShapeDtypeStruct((t, d), F32),
        grid=(t // tm,),
        in_specs=[row, _resident(nw.shape), _resident(wg.shape), _resident(wu.shape),
                  _resident(wd.shape), _resident(fw.shape)],
        out_specs=row,
        compiler_params=_params(("parallel",)),
        name="ffn_last",
    )(x, nw, wg, wu, wd, fw)


def _proj_body(x_ref, nw_ref, wrqk_ref, wrv_ref, wdq_ref, wdk_ref, wdv_ref, wmq_ref,
               rcos_ref, rsin_ref, dtab_ref, dqn_ref, dkn_ref, mqn_ref, *rest, n_cast):
    cast_in, cast_out = rest[:n_cast], rest[n_cast + 7:]
    rq_ref, rk_ref, rv_ref, dq_ref, dk_ref, dv_ref, mq_ref = rest[n_cast:n_cast + 7]
    lane = lax.broadcasted_iota(jnp.int32, (1, LANES), 1)
    half = ROT_DIM // 2
    low = lane < DIFF_DK
    first = (lane % DIFF_DK) < half
    for r in range(0, x_ref.shape[0], PROJ_SUBROWS):
        _proj_rows(slice(r, r + PROJ_SUBROWS), lane, half, low, first, x_ref, nw_ref, wrqk_ref, wrv_ref, wdq_ref,
                   wdk_ref, wdv_ref, wmq_ref, rcos_ref, rsin_ref, dtab_ref, dqn_ref, dkn_ref, mqn_ref,
                   rq_ref, rk_ref, rv_ref, dq_ref, dk_ref, dv_ref, mq_ref)
    _cast_chunks(cast_in, cast_out)


def _proj_rows(rows, lane, half, low, first, x_ref, nw_ref, wrqk_ref, wrv_ref, wdq_ref, wdk_ref, wdv_ref, wmq_ref,
               rcos_ref, rsin_ref, dtab_ref, dqn_ref, dkn_ref, mqn_ref,
               rq_ref, rk_ref, rv_ref, dq_ref, dk_ref, dv_ref, mq_ref):
    h = _rms(x_ref[rows, :], nw_ref[...]).astype(BF16)
    rcos = rcos_ref[rows, :]
    rsin = rsin_ref[rows, :]

    dtab = dtab_ref[:, rows].T
    c = jnp.where(lane < half, dtab, 0.0)
    c = c + pltpu.roll(c, half, 1)
    c = c + pltpu.roll(c, DIFF_DK, 1)
    dcos = jnp.where((lane % DIFF_DK) < ROT_DIM, c, 1.0)
    sn = jnp.where((lane >= half) & (lane < ROT_DIM), dtab, 0.0)
    sn = sn - pltpu.roll(sn, LANES - half, 1)
    dsin = sn + pltpu.roll(sn, DIFF_DK, 1)

    def ret_rotate(out_ref, scale):
        def epilogue(y):
            for hd in range(RET_HEADS):
                ye = y[:, hd * RET_DK:hd * RET_DK + LANES]
                yo = y[:, hd * RET_DK + LANES:(hd + 1) * RET_DK]
                out_ref[rows, hd * RET_DK:hd * RET_DK + LANES] = ((ye * rcos - yo * rsin) * scale).astype(BF16)
                out_ref[rows, hd * RET_DK + LANES:(hd + 1) * RET_DK] = ((yo * rcos + ye * rsin) * scale).astype(BF16)
        return epilogue

    def store(out_ref):
        def epilogue(y):
            out_ref[rows, :] = y.astype(BF16)
        return epilogue

    def diff_norm_rope(out_ref, n_ref, scale):
        def epilogue(y):
            gain = n_ref[...]
            for hd in range(DIFF_HEADS):
                yh = y[:, hd * LANES:(hd + 1) * LANES]
                sq = yh * yh
                ss_lo = jnp.sum(jnp.where(low, sq, 0.0), axis=-1, keepdims=True)
                ss_hi = jnp.sum(jnp.where(low, 0.0, sq), axis=-1, keepdims=True)
                r = jnp.where(low, lax.rsqrt(ss_lo * (1.0 / DIFF_DK) + EPS),
                              lax.rsqrt(ss_hi * (1.0 / DIFF_DK) + EPS))
                yn = yh * r * gain
                partner = jnp.where(first, pltpu.roll(yn, LANES - half, 1), pltpu.roll(yn, half, 1))
                out_ref[rows, hd * LANES:(hd + 1) * LANES] = ((yn * dcos + partner * dsin) * scale).astype(BF16)
        return epilogue

    def mem_norm(y):
        gain = mqn_ref[...]
        for hd in range(MEM_HEADS):
            yh = y[:, hd * MEM_DH:(hd + 1) * MEM_DH]
            mq_ref[rows, hd * MEM_DH:(hd + 1) * MEM_DH] = (_rms(yh) * gain).astype(BF16)

    stages = (
        (lambda: wdq_ref[...], diff_norm_rope(dq_ref, dqn_ref, LOG2E * DIFF_DK ** -0.5)),
        (lambda: wrqk_ref[:, 0:RET_QK_W], ret_rotate(rq_ref, 1.0)),
        (lambda: wdk_ref[...], diff_norm_rope(dk_ref, dkn_ref, 1.0)),
        (lambda: wrqk_ref[:, RET_QK_W:2 * RET_QK_W], ret_rotate(rk_ref, RET_DK ** -0.5)),
        (lambda: wmq_ref[...], mem_norm),
        (lambda: wdv_ref[...], store(dv_ref)),
        (lambda: wrv_ref[...], store(rv_ref)),
    )
    y_next = _dot(h, stages[0][0]())
    for i, (_, epilogue) in enumerate(stages):
        y = y_next
        if i + 1 < len(stages):
            y_next = _dot(h, stages[i + 1][0]())
        epilogue(y)


def _proj(x1, nw, w_rqk, w_in, rcos, rsin, dtab, dqn, dkn, mqn, cast, *, tm=512):
    t, d = x1.shape
    cast_specs, cast_shapes = _cast_specs(cast, t // tm)
    widths = (RET_QK_W, RET_QK_W, RET_V_W, DIFF_QK_W, DIFF_QK_W, DIFF_V_W, MEM_Q_W)

    def row(w):
        return pl.BlockSpec((tm, w), lambda i: (i, 0))

    outs = pl.pallas_call(
        functools.partial(_proj_body, n_cast=len(cast)),
        out_shape=(*(jax.ShapeDtypeStruct((t, w), BF16) for w in widths), *cast_shapes),
        grid=(t // tm,),
        in_specs=[row(d), _resident(nw.shape), _resident(w_rqk.shape),
                  _resident_cols(d, RET_V_W, 1),
                  _resident_cols(d, DIFF_QK_W, 6), _resident_cols(d, DIFF_QK_W, 7),
                  _resident_cols(d, DIFF_V_W, 8), _resident_cols(d, MEM_Q_W, 9),
                  row(LANES), row(LANES), pl.BlockSpec((LANES, tm), lambda i: (0, i)),
                  _resident(dqn.shape), _resident(dkn.shape), _resident(mqn.shape), *cast_specs],
        out_specs=(*(row(w) for w in widths), *cast_specs),
        compiler_params=_params(("arbitrary",)),
        name="proj",
    )(x1, nw, w_rqk, w_in, w_in, w_in, w_in, w_in, rcos, rsin, dtab, dqn, dkn, mqn, *cast)
    return outs[:len(widths)], outs[len(widths):]


def _memkv_body(m_ref, nw_ref, w_ref, kn_ref, pos_ref, rinv_ref, *rest, n_cast):
    cast_in, (mk_ref, mv_ref, rcos_ref, rsin_ref), cast_out = (
        rest[:n_cast], rest[n_cast:n_cast + 4], rest[n_cast + 4:])
    _cast_chunks(cast_in, cast_out)
    ang = pos_ref[...] * rinv_ref[...]
    rcos_ref[...] = jnp.cos(ang)
    rsin_ref[...] = jnp.sin(ang)
    hm = _rms(m_ref[...], nw_ref[...]).astype(BF16)
    k = _dot(hm, w_ref[:, 0:MEM_Q_W].astype(BF16))
    gain = kn_ref[...] * (MEM_DH ** -0.5)
    for hd in range(MEM_HEADS):
        kh = k[:, hd * MEM_DH:(hd + 1) * MEM_DH]
        mk_ref[:, hd * MEM_DH:(hd + 1) * MEM_DH] = (_rms(kh) * gain).astype(BF16)
    mv_ref[...] = _dot(hm, w_ref[:, MEM_Q_W:2 * MEM_Q_W].astype(BF16)).astype(BF16)


def _memkv(mem2d, nw, w_kv, kn, pos, rinv, cast, *, tm=256):
    t, d = mem2d.shape
    steps = t // tm
    tok = pos.shape[0] // steps
    row = pl.BlockSpec((tm, d), lambda i: (i, 0))
    out = pl.BlockSpec((tm, MEM_Q_W), lambda i: (i, 0))
    tab = pl.BlockSpec((tok, LANES), lambda i: (i, 0))
    cast_specs, cast_shapes = _cast_specs(cast, steps)
    outs = pl.pallas_call(
        functools.partial(_memkv_body, n_cast=len(cast)),
        out_shape=(jax.ShapeDtypeStruct((t, MEM_Q_W), BF16),) * 2
        + (jax.ShapeDtypeStruct((pos.shape[0], LANES), F32),) * 2 + tuple(cast_shapes),
        grid=(steps,),
        in_specs=[row, _resident(nw.shape), _resident(w_kv.shape), _resident(kn.shape),
                  pl.BlockSpec((tok, 1), lambda i: (i, 0)), _resident(rinv.shape), *cast_specs],
        out_specs=(out, out, tab, tab, *cast_specs),
        compiler_params=_params(("arbitrary",)),
        name="memkv",
    )(mem2d, nw, w_kv, kn, pos, rinv, *cast)
    return outs[0], outs[1], outs[2], outs[3], outs[4:]


def _ret_body(q_ref, k_ref, v_ref, dmat_ref, qdec_ref, kdec_ref, cdec_ref, o_ref, state_ref, *, seq):
    nblk = seq // RET_BLOCK

    def local(item):
        n, hd = item
        rows = slice(n * RET_BLOCK, (n + 1) * RET_BLOCK)
        q = q_ref[rows, hd * RET_DK:(hd + 1) * RET_DK]
        k = k_ref[rows, hd * RET_DK:(hd + 1) * RET_DK]
        v = v_ref[rows, hd * RET_DV:(hd + 1) * RET_DV]
        s = _dot_nt(q, k) * dmat_ref[hd]
        o = _dot(s.astype(BF16), v)
        upd = _dot_tn((k.astype(F32) * kdec_ref[hd]).astype(BF16), v)
        qd = (q.astype(F32) * qdec_ref[hd]).astype(BF16) if n else None
        return o, upd, qd

    items = [(n, hd) for n in range(nblk) for hd in range(RET_HEADS_PER_STEP)]
    nxt = local(items[0])
    for step, (n, hd) in enumerate(items):
        rows = slice(n * RET_BLOCK, (n + 1) * RET_BLOCK)
        o, upd, qd = nxt
        if step + 1 < len(items):
            nxt = local(items[step + 1])
        if n == 0:
            state_ref[hd] = upd
        else:
            st = state_ref[hd]
            o = o + _dot(qd, st.astype(BF16))
            state_ref[hd] = st * cdec_ref[hd] + upd
        o_ref[rows, hd * RET_DV:(hd + 1) * RET_DV] = _rms(o).astype(BF16)


def _diff_body(q_ref, k_ref, v_ref, lamv_ref, subln_ref, o_ref, vx_ref, *, seq):
    lv = lamv_ref[...]
    lam = (jnp.exp(jnp.sum(lv[0:1] * lv[1:2], axis=-1, keepdims=True))
           - jnp.exp(jnp.sum(lv[2:3] * lv[3:4], axis=-1, keepdims=True)) + LAM_INIT)
    gain = subln_ref[...] * (1.0 - LAM_INIT)
    tq = DIFF_QBLOCK
    low = lax.broadcasted_iota(jnp.int32, (1, LANES), 1) < DIFF_DK
    rchunk = lax.broadcasted_iota(jnp.int32, (tq, tq), 0) // CHUNK
    cchunk = lax.broadcasted_iota(jnp.int32, (tq, tq), 1) // CHUNK
    visible = cchunk <= rchunk
    zero = jnp.zeros((), BF16)

    nq = seq // tq
    for hd in range(DIFF_HEADS_PER_STEP):
        vx_ref[hd, :, 0:DIFF_DV] = v_ref[:, hd * DIFF_DV:(hd + 1) * DIFF_DV]
        vx_ref[hd, :, DIFF_DV:2 * DIFF_DV] = jnp.ones((seq, DIFF_DV), BF16)

    def scores(item):
        hd, qi, mp = item
        cols = slice(hd * LANES, (hd + 1) * LANES)
        q = q_ref[qi * tq:(qi + 1) * tq, cols]
        qm = jnp.where(low, zero, q) if mp else jnp.where(low, q, zero)
        sd = jnp.where(visible, _dot_nt(qm, k_ref[qi * tq:(qi + 1) * tq, cols]), -jnp.inf)
        sp = _dot_nt(qm, k_ref[0:qi * tq, cols]) if qi else None
        return sd, sp

    items = [(hd, qi, mp) for hd in range(DIFF_HEADS_PER_STEP) for qi in reversed(range(nq)) for mp in (0, 1)]
    ahead = [scores(item) for item in items[:DIFF_LOOKAHEAD]]
    o_map0 = None
    for step, (hd, qi, mp) in enumerate(items):
        rows = slice(qi * tq, (qi + 1) * tq)
        past = qi * tq
        sd, sp = ahead.pop(0)
        if step + DIFF_LOOKAHEAD < len(items):
            ahead.append(scores(items[step + DIFF_LOOKAHEAD]))
        m = jnp.max(sd, axis=-1, keepdims=True)
        if past:
            m = jnp.maximum(m, jnp.max(sp, axis=-1, keepdims=True))
        ol = _dot(jnp.exp2(sd - m).astype(BF16), vx_ref[hd, rows, :])
        if past:
            ol = ol + _dot(jnp.exp2(sp - m).astype(BF16), vx_ref[hd, 0:past, :])
        o = ol[:, 0:DIFF_DV] / ol[:, DIFF_DV:2 * DIFF_DV]
        if mp == 0:
            o_map0 = o
        else:
            o_ref[rows, hd * DIFF_DV:(hd + 1) * DIFF_DV] = (_rms(o_map0 - lam * o) * gain).astype(BF16)


def _mixers_body(rq_ref, rk_ref, rv_ref, dmat_ref, qdec_ref, kdec_ref, cdec_ref,
                 dq_ref, dk_ref, dv_ref, lamv_ref, subln_ref, ro_ref, do_ref, state_ref, vx_ref, *, seq):
    _ret_body(rq_ref, rk_ref, rv_ref, dmat_ref, qdec_ref, kdec_ref, cdec_ref, ro_ref, state_ref, seq=seq)
    _diff_body(dq_ref, dk_ref, dv_ref, lamv_ref, subln_ref, do_ref, vx_ref, seq=seq)


def _mixers(rq, rk, rv, dmat, qdec, kdec, cdec, dq, dk, dv, lamv, subln, *, batch, seq):
    steps = RET_HEADS // RET_HEADS_PER_STEP
    assert steps == DIFF_HEADS // DIFF_HEADS_PER_STEP
    qk = pl.BlockSpec((seq, RET_HEADS_PER_STEP * RET_DK), lambda b, h: (b, h))
    vv = pl.BlockSpec((seq, RET_HEADS_PER_STEP * RET_DV), lambda b, h: (b, h))
    blk = pl.BlockSpec((seq, DIFF_HEADS_PER_STEP * LANES), lambda b, h: (b, h))

    def per_head(shape):
        return pl.BlockSpec((RET_HEADS_PER_STEP,) + shape[1:], lambda b, h: (h, 0, 0))

    return pl.pallas_call(
        functools.partial(_mixers_body, seq=seq),
        out_shape=(jax.ShapeDtypeStruct(rv.shape, BF16), jax.ShapeDtypeStruct(dv.shape, BF16)),
        grid=(batch, steps),
        in_specs=[qk, qk, vv, per_head(dmat.shape), per_head(qdec.shape), per_head(kdec.shape),
                  per_head(cdec.shape), blk, blk, blk, _resident(lamv.shape), _resident(subln.shape)],
        out_specs=(vv, blk),
        scratch_shapes=[pltpu.VMEM((RET_HEADS_PER_STEP, RET_DK, RET_DV), F32),
                        pltpu.VMEM((DIFF_HEADS_PER_STEP, seq, 2 * DIFF_DV), BF16)],
        compiler_params=_params(("parallel", "parallel")),
        name="mixers",
    )(rq, rk, rv, dmat, qdec, kdec, cdec, dq, dk, dv, lamv, subln)


def _merge_body(x_ref, ro_ref, do_ref, mq_ref, mk_ref, mv_ref, nw_ref, wrg_ref, wg0_ref, wg1_ref, wg2_ref, bg_ref,
                wro_ref, wdo_ref, wmo_ref, wout_ref, o_ref):
    for r in range(0, x_ref.shape[0], MERGE_SUBROWS):
        rows = slice(r, r + MERGE_SUBROWS)
        heads = [slice(hd * MEM_DH, (hd + 1) * MEM_DH) for hd in range(MEM_HEADS)]
        scores = [_dot_nt(mq_ref[rows, cols], mk_ref[:, cols]) for cols in heads]
        x = x_ref[rows, :]
        h = _rms(x, nw_ref[...]).astype(BF16)
        rg = _dot(h, wrg_ref[...])
        probs = []
        for s in scores:
            p = jnp.exp(s - jnp.max(s, axis=-1, keepdims=True))
            probs.append((p * (1.0 / jnp.sum(p, axis=-1, keepdims=True))).astype(BF16))
        g = [jax.nn.sigmoid(_dot(h, w_ref[...]) + bg_ref[:, i * D_MODEL:(i + 1) * D_MODEL])
             for i, w_ref in enumerate((wg0_ref, wg1_ref, wg2_ref))]
        mem_heads = [_dot(p, mv_ref[:, cols]).astype(BF16) for p, cols in zip(probs, heads)]
        ro = (ro_ref[rows, :].astype(F32) * (rg * jax.nn.sigmoid(rg))).astype(BF16)
        ret_out = _dot(ro, wro_ref[...])
        mem_out = None
        for oh, cols in zip(mem_heads, heads):
            part = _dot(oh, wmo_ref[cols, :])
            mem_out = part if mem_out is None else mem_out + part
        diff_out = _dot(do_ref[rows, :], wdo_ref[...])
        merged = g[0] * ret_out + g[1] * diff_out + g[2] * mem_out
        o_ref[rows, :] = x + _dot(merged.astype(BF16), wout_ref[...])


def _merge(x1, ro, do, mq, mk, mv, nw, w_in, bg, wro, wdo, wmo, wout, *, seq, mem_len, tm=512):
    t, d = x1.shape
    per_b = seq // tm

    def row(w):
        return pl.BlockSpec((tm, w), lambda i: (i, 0))

    memb = pl.BlockSpec((mem_len, MEM_Q_W), lambda i: (i // per_b, 0))
    return pl.pallas_call(
        _merge_body,
        out_shape=jax.ShapeDtypeStruct((t, d), F32),
        grid=(t // tm,),
        in_specs=[row(d), row(RET_V_W), row(DIFF_V_W), row(MEM_Q_W), memb, memb,
                  _resident(nw.shape), _resident_cols(d, RET_V_W, 2),
                  _resident_cols(d, D_MODEL, QKV_W // D_MODEL),
                  _resident_cols(d, D_MODEL, QKV_W // D_MODEL + 1),
                  _resident_cols(d, D_MODEL, QKV_W // D_MODEL + 2), _resident(bg.shape), _resident(wro.shape),
                  _resident(wdo.shape), _resident(wmo.shape), _resident(wout.shape)],
        out_specs=row(d),
        compiler_params=_params(("parallel",)),
        name="merge",
    )(x1, ro, do, mq, mk, mv, nw, w_in, w_in, w_in, w_in, bg, wro, wdo, wmo, wout)


def _deinterleave_matrix():
    j = jnp.arange(RET_DK)
    src = jnp.where(j < RET_DK // 2, 2 * j, 2 * (j - RET_DK // 2) + 1)
    return (jnp.arange(RET_DK)[:, None] == src[None, :]).astype(BF16)


def _rotation_tables(positions):
    pos = positions.astype(F32).reshape(-1, 1)
    ret_inv = (1.0 / (RET_THETA_BASE ** jnp.linspace(0.0, 1.0, RET_DK // 2, dtype=F32))).reshape(1, -1)
    rope_inv = 1.0 / (ROPE_THETA ** (jnp.arange(0, ROT_DIM, 2, dtype=F32) / ROT_DIM))
    d_ang = rope_inv[:, None] * pos.reshape(1, -1)
    dtab = jnp.concatenate([jnp.cos(d_ang), jnp.sin(d_ang), jnp.zeros((LANES - ROT_DIM, pos.shape[0]), F32)], axis=0)
    return pos, ret_inv, dtab


def _decay_tables():
    log_g = jnp.log(1.0 - 2.0 ** (-5.0 - jnp.arange(RET_HEADS, dtype=F32)))
    idx = jnp.arange(RET_BLOCK, dtype=F32)
    dist = jnp.abs(idx[:, None] - idx[None, :])
    chunk = jnp.arange(RET_BLOCK) // CHUNK
    visible = chunk[None, :] <= chunk[:, None]
    dmat = jnp.where(visible[None], jnp.exp(log_g[:, None, None] * dist[None]), 0.0)
    qdec = jnp.exp(log_g[:, None] * (idx[None, :] + 1.0))[..., None]
    kdec = jnp.exp(log_g[:, None] * (RET_BLOCK - 1.0 - idx[None, :]))[..., None]
    cdec = jnp.broadcast_to(jnp.exp(log_g * RET_BLOCK)[:, None, None], (RET_HEADS, 1, RET_DV))
    return dmat, qdec, kdec, cdec


def kernel(x, mem, positions, ffn1_norm, ffn1_w_gate, ffn1_w_up, ffn1_w_down, mix_norm, w_in, b_gate, ret_w_o, diff_q_norm, diff_k_norm, diff_lambda_q1, diff_lambda_k1, diff_lambda_q2, diff_lambda_k2, diff_subln, diff_w_o, mem_norm, mem_w_kv, mem_q_norm, mem_k_norm, mem_w_o, w_out, ffn2_norm, ffn2_w_gate, ffn2_w_up, ffn2_w_down, final_norm):
    batch, seq, d = x.shape
    mem_len = mem.shape[1]
    assert d == D_MODEL and seq % RET_BLOCK == 0 and seq % DIFF_QBLOCK == 0
    assert w_in.shape[0] == 1, "single-layer trunk"
    vec = lambda a: a.reshape(1, -1)

    pos, ret_inv, dtab = _rotation_tables(positions)
    dmat, qdec, kdec, cdec = _decay_tables()
    perm = _deinterleave_matrix()
    dqn = jnp.tile(vec(diff_q_norm[0]), (1, 2))
    dkn = jnp.tile(vec(diff_k_norm[0]), (1, 2))
    lamv = jnp.stack([diff_lambda_q1[0], diff_lambda_k1[0], diff_lambda_q2[0], diff_lambda_k2[0]]).astype(F32)

    mk, mv, rcos, rsin, (w1g, w1u, w1d) = _memkv(
        mem.reshape(batch * mem_len, d), vec(mem_norm[0]), mem_w_kv[0], vec(mem_k_norm[0]), pos, ret_inv,
        (ffn1_w_gate[0], ffn1_w_up[0], ffn1_w_down[0]))
    x1, w_rqk, (wi, w_ro, w_do, w_mo, w_o) = _ffn_first(
        x.reshape(batch * seq, d), vec(ffn1_norm[0]), w1g, w1u, w1d, perm,
        (w_in[0], ret_w_o[0], diff_w_o[0], mem_w_o[0], w_out[0]))
    (rq, rk, rv, dq, dk, dv, mq), (w2g, w2u, w2d) = _proj(
        x1, vec(mix_norm[0]), w_rqk, wi, rcos, rsin, dtab, dqn, dkn, vec(mem_q_norm[0]),
        (ffn2_w_gate[0], ffn2_w_up[0], ffn2_w_down[0]))
    ro, do = _mixers(rq, rk, rv, dmat, qdec, kdec, cdec, dq, dk, dv, lamv, vec(diff_subln[0]), batch=batch, seq=seq)
    x2 = _merge(x1, ro, do, mq, mk, mv, vec(mix_norm[0]), wi, vec(b_gate[0]), w_ro, w_do, w_mo, w_o,
                seq=seq, mem_len=mem_len)
    x3 = _ffn_last(x2, vec(ffn2_norm[0]), w2g, w2u, w2d, vec(final_norm[0]))
    return x3.reshape(batch, seq, d)
```
